```python
import jax, jax.numpy as jnp
from jax import lax
import numpy as np

D_MODEL = 2048
BATCH = 4
SEQ = 8192
DEPTH = 2
DEC_BATCH = 2
DEC_SEQ = 16384
PAST_LEN = 128

HEAD_DIM = 128
RET_HEADS = 4
NA_HEADS = 4
GA_HEADS = 8
GA_KV_HEADS = 2
RET_W = RET_HEADS * HEAD_DIM
NA_W = NA_HEADS * HEAD_DIM
GA_W = GA_HEADS * HEAD_DIM
GA_KV_W = GA_KV_HEADS * HEAD_DIM
MIX_W = RET_W + NA_W + GA_W
IN_PROJ_W = 4 * RET_W + 3 * NA_W + GA_W + 2 * GA_KV_W
RET_CHUNK = 128
GRID_W = 64
NA_WIN_R = 8
NA_WIN_C = 16
NA_QROWS = 2
ATTN_QBLK = 128
ROPE_BASE = 10000.0
N_EXPERTS = 64
N_GROUPS = 8
TOPK_GROUPS = 4
TOP_K = 8
EXPERT_FF = 512
SHARED_FF = 512
ROUTED_SCALE = 2.5
MOE_BLK = 128
N_MOD = 6
EPS = 1e-6

kernel_name = "hybrid_bidir_retention_natten_gqa_moe_encoder"


def rms_norm(x, g):
    xf = x.astype(jnp.float32)
    y = xf * lax.rsqrt(jnp.mean(xf * xf, axis=-1, keepdims=True) + EPS)
    return (y * g.astype(jnp.float32)).astype(x.dtype)


def rope_angles(pos, n_pairs):
    inv = ROPE_BASE ** (-jnp.arange(n_pairs, dtype=jnp.float32) / n_pairs)
    ang = pos.astype(jnp.float32)[:, None] * inv[None, :]
    return jnp.cos(ang), jnp.sin(ang)


def apply_rope(x, cos, sin):
    x1, x2 = jnp.split(x, 2, axis=-1)
    c = cos[None, :, None, :]
    s = sin[None, :, None, :]
    return jnp.concatenate([x1 * c - x2 * s, x2 * c + x1 * s], axis=-1)


def axial_rope(x, row_cs, col_cs):
    xr, xc = jnp.split(x, 2, axis=-1)
    return jnp.concatenate([apply_rope(xr, *row_cs), apply_rope(xc, *col_cs)], axis=-1)


def retention_one_direction(q, k, v, log_gamma, strict):
    b, t, h, d = q.shape
    nc = t // RET_CHUNK
    q = q.reshape(b, nc, RET_CHUNK, h, d)
    k = k.reshape(b, nc, RET_CHUNK, h, d)
    v = v.reshape(b, nc, RET_CHUNK, h, d)
    idx = jnp.arange(RET_CHUNK, dtype=jnp.float32)
    diff = idx[:, None] - idx[None, :]
    mask = (diff > 0) if strict else (diff >= 0)
    decay = jnp.where(mask[None], jnp.exp(log_gamma[:, None, None] * jnp.maximum(diff, 0.0)[None]), 0.0)
    scores = jnp.einsum('bnihd,bnjhd->bnhij', q, k) * decay[None, None]
    o_inner = jnp.einsum('bnhij,bnjhe->bnihe', scores, v)
    k_dec = k * jnp.exp(log_gamma[None, :] * (RET_CHUNK - 1.0 - idx)[:, None])[:, :, None]
    u = jnp.einsum('bnjhd,bnjhe->nbhde', k_dec, v)
    chunk_decay = jnp.exp(log_gamma * RET_CHUNK)[None, :, None, None]

    def step(state, u_i):
        return state * chunk_decay + u_i, state

    _, s_prev = lax.scan(step, jnp.zeros((b, h, d, d), jnp.float32), u)
    q_dec = q * jnp.exp(log_gamma[None, :] * (idx + 1.0)[:, None])[:, :, None]
    o_cross = jnp.einsum('bnihd,nbhde->bnihe', q_dec, s_prev)
    return (o_inner + o_cross).reshape(b, t, h, d)


def retention_mixer(q, k, v, gate, cos, sin, dec_fwd, dec_bwd, g_norm):
    b, t, _ = q.shape
    shp = (b, t, RET_HEADS, HEAD_DIM)
    qf = apply_rope(q.reshape(shp).astype(jnp.float32), cos, sin)
    kf = apply_rope(k.reshape(shp).astype(jnp.float32), cos, sin) * (HEAD_DIM ** -0.5)
    vf = v.reshape(shp).astype(jnp.float32)
    lg_f = jax.nn.log_sigmoid(dec_fwd.astype(jnp.float32))
    lg_b = jax.nn.log_sigmoid(dec_bwd.astype(jnp.float32))
    fwd = retention_one_direction(qf, kf, vf, lg_f, strict=False)
    bwd = jnp.flip(retention_one_direction(jnp.flip(qf, 1), jnp.flip(kf, 1), jnp.flip(vf, 1), lg_b, strict=True), 1)
    o = rms_norm(fwd + bwd, g_norm.reshape(RET_HEADS, HEAD_DIM))
    return (jax.nn.silu(gate.astype(jnp.float32)) * o.reshape(b, t, RET_W)).astype(q.dtype)


def neighbourhood_mixer(q, k, v, g_q, g_k, rpb):
    b, t, _ = q.shape
    rows = t // GRID_W
    wr = min(NA_WIN_R, rows)
    grid = (b, rows, GRID_W, NA_HEADS, HEAD_DIM)
    qg = rms_norm(q.reshape(grid), g_q)
    kg = rms_norm(k.reshape(grid), g_k)
    vg = v.reshape(grid)
    col = jnp.arange(GRID_W)
    col_idx = jnp.clip(col - NA_WIN_C // 2, 0, GRID_W - NA_WIN_C)[:, None] + jnp.arange(NA_WIN_C)[None, :]
    dc = col_idx - col[:, None] + (NA_WIN_C - 1)
    rpb = rpb.astype(jnp.float32)

    def block(blk):
        r = blk * NA_QROWS + jnp.arange(NA_QROWS)
        row_idx = jnp.clip(r - wr // 2, 0, rows - wr)[:, None] + jnp.arange(wr)[None, :]
        dr = row_idx - r[:, None] + (NA_WIN_R - 1)
        qb = lax.dynamic_slice_in_dim(qg, blk * NA_QROWS, NA_QROWS, axis=1)
        kw = kg[:, row_idx][:, :, :, col_idx]
        vw = vg[:, row_idx][:, :, :, col_idx]
        s = jnp.einsum('brchd,bricjhd->bhrcij', qb, kw).astype(jnp.float32) * (HEAD_DIM ** -0.5)
        s = s + rpb[:, dr[:, None, :, None], dc[None, :, None, :]][None]
        p = jax.nn.softmax(s.reshape(s.shape[:4] + (wr * NA_WIN_C,)), axis=-1).reshape(s.shape)
        return jnp.einsum('bhrcij,bricjhd->brchd', p.astype(vw.dtype), vw)

    o = lax.map(block, jnp.arange(rows // NA_QROWS))
    return jnp.moveaxis(o, 0, 1).reshape(b, t, NA_W)


def grid_attention_mixer(q, k, v, g_q, g_k, row_cs, col_cs):
    b, t, _ = q.shape
    groups = GA_HEADS // GA_KV_HEADS
    qh = rms_norm(q.reshape(b, t, GA_HEADS, HEAD_DIM), g_q).astype(jnp.float32)
    kh = rms_norm(k.reshape(b, t, GA_KV_HEADS, HEAD_DIM), g_k).astype(jnp.float32)
    vh = v.reshape(b, t, GA_KV_HEADS, HEAD_DIM)
    qh = (axial_rope(qh, row_cs, col_cs) * (HEAD_DIM ** -0.5)).astype(q.dtype)
    kh = axial_rope(kh, row_cs, col_cs).astype(k.dtype)
    qb = jnp.moveaxis(qh.reshape(b, t // ATTN_QBLK, ATTN_QBLK, GA_KV_HEADS, groups, HEAD_DIM), 1, 0)

    def block(qi):
        s = jnp.einsum('bqhgd,bkhd->bhgqk', qi, kh).astype(jnp.float32)
        p = jax.nn.softmax(s, axis=-1)
        return jnp.einsum('bhgqk,bkhd->bqhgd', p.astype(vh.dtype), vh)

    o = lax.map(block, qb)
    return jnp.moveaxis(o, 0, 1).reshape(b, t, GA_W)


def moe_ffn(h, w_router, router_bias, w_eg, w_eu, w_ed, w_sg, w_su, w_sd):
    b, t, d = h.shape
    n = b * t
    xf = h.reshape(n, d)
    scores = jax.nn.sigmoid(xf.astype(jnp.float32) @ w_router.astype(jnp.float32))
    choice = scores + router_bias.astype(jnp.float32)
    group_score = lax.top_k(choice.reshape(n, N_GROUPS, N_EXPERTS // N_GROUPS), 2)[0].sum(-1)
    _, top_groups = lax.top_k(group_score, TOPK_GROUPS)
    group_mask = jax.nn.one_hot(top_groups, N_GROUPS, dtype=jnp.float32).sum(1) > 0
    expert_mask = jnp.repeat(group_mask, N_EXPERTS // N_GROUPS, axis=1)
    _, top_e = lax.top_k(jnp.where(expert_mask, choice, -jnp.inf), TOP_K)
    w = jnp.take_along_axis(scores, top_e, axis=1)
    w = w / jnp.sum(w, axis=-1, keepdims=True) * ROUTED_SCALE
    nk = n * TOP_K
    flat_e = top_e.reshape(nk).astype(jnp.int32)
    order = jnp.argsort(flat_e)
    sorted_e = flat_e[order]
    sorted_tok = (order // TOP_K).astype(jnp.int32)
    sorted_w = w.reshape(nk)[order]
    counts = jnp.zeros((N_EXPERTS,), jnp.int32).at[flat_e].add(1)
    starts = jnp.cumsum(counts) - counts
    padded = (counts + MOE_BLK - 1) // MOE_BLK * MOE_BLK
    pad_ends = jnp.cumsum(padded)
    dest = (pad_ends - padded)[sorted_e] + jnp.arange(nk, dtype=jnp.int32) - starts[sorted_e]
    n_blocks = (nk + N_EXPERTS * (MOE_BLK - 1) + MOE_BLK - 1) // MOE_BLK
    row_tok = jnp.full((n_blocks * MOE_BLK,), n, jnp.int32).at[dest].set(sorted_tok)
    row_w = jnp.zeros((n_blocks * MOE_BLK,), jnp.float32).at[dest].set(sorted_w)
    block_expert = jnp.minimum(
        jnp.searchsorted(pad_ends, jnp.arange(n_blocks, dtype=jnp.int32) * MOE_BLK, side='right'),
        N_EXPERTS - 1)
    x_pad = jnp.concatenate([xf, jnp.zeros((1, d), xf.dtype)], axis=0)

    def expert_block(acc, blk):
        tok, wt, e = blk
        xb = x_pad[tok]
        hb = jax.nn.silu(xb @ w_eg[e]) * (xb @ w_eu[e])
        yb = (hb @ w_ed[e]).astype(jnp.float32) * wt[:, None]
        return acc.at[tok].add(yb), None

    acc, _ = lax.scan(expert_block, jnp.zeros((n + 1, d), jnp.float32),
                      (row_tok.reshape(n_blocks, MOE_BLK), row_w.reshape(n_blocks, MOE_BLK), block_expert))
    shared = (jax.nn.silu(xf @ w_sg) * (xf @ w_su)) @ w_sd
    return (acc[:n] + shared.astype(jnp.float32)).reshape(b, t, d).astype(h.dtype)


def encoder_layer(x, c, ropes, p):
    (w_ada, b_ada, g1, w_in, dec_f, dec_b, ret_gn, na_qn, na_kn, na_rpb, na_on,
     ga_qn, ga_kn, ga_on, w_out, g2, w_router, router_bias, w_eg, w_eu, w_ed, w_sg, w_su, w_sd) = p
    ret_cs, row_cs, col_cs = ropes
    mod = (jax.nn.silu(c) @ w_ada + b_ada)[:, None, :]
    sh1, sc1, gt1, sh2, sc2, gt2 = jnp.split(mod, N_MOD, axis=-1)
    h = rms_norm(x, g1) * (1.0 + sc1) + sh1
    proj = jnp.einsum('btd,de->bte', h, w_in)
    sizes = (RET_W,) * 4 + (NA_W,) * 3 + (GA_W, GA_KV_W, GA_KV_W)
    (rq, rk, rv, rg, nq, nk, nv, gq, gk, gv) = jnp.split(
        proj, [int(i) for i in np.cumsum(sizes)[:-1]], axis=-1)
    y_ret = retention_mixer(rq, rk, rv, rg, ret_cs[0], ret_cs[1], dec_f, dec_b, ret_gn)
    y_na = rms_norm(neighbourhood_mixer(nq, nk, nv, na_qn, na_kn, na_rpb), na_on)
    y_ga = rms_norm(grid_attention_mixer(gq, gk, gv, ga_qn, ga_kn, row_cs, col_cs), ga_on)
    y = jnp.concatenate([y_ret, y_na, y_ga], axis=-1) @ w_out
    x = x + gt1 * y
    h = rms_norm(x, g2) * (1.0 + sc2) + sh2
    return x + gt2 * moe_ffn(h, w_router, router_bias, w_eg, w_eu, w_ed, w_sg, w_su, w_sd)


def encoder_trunk(x, c, layers):
    t = x.shape[1]
    pos = jnp.arange(t, dtype=jnp.int32)
    ropes = (rope_angles(pos, HEAD_DIM // 2),
             rope_angles(pos // GRID_W, HEAD_DIM // 4),
             rope_angles(pos % GRID_W, HEAD_DIM // 4))
    for l in range(DEPTH):
        x = encoder_layer(x, c, ropes, layers[l])
    return x


def setup_inputs(seed: int = 0) -> dict:
    key = jax.random.key(seed)
    k = jax.random.split(key, 28)
    f32 = jnp.float32

    def nrm(kk, shape, scale):
        return jax.random.normal(kk, shape, f32) * scale

    def gain(kk, shape):
        return 1.0 + 0.05 * jax.random.normal(kk, shape, f32)

    ret_logit = jnp.log(2.0 ** (5.0 + jnp.arange(RET_HEADS, dtype=f32)) - 1.0)
    return {
        "x_prompt": nrm(k[0], (BATCH, SEQ, D_MODEL), 1.0),
        "x_sample": nrm(k[1], (DEC_BATCH, DEC_SEQ, D_MODEL), 1.0),
        "c_prompt": nrm(k[2], (BATCH, D_MODEL), 1.0),
        "c_sample": nrm(k[3], (DEC_BATCH, D_MODEL), 1.0),
        "w_ada": nrm(k[4], (DEPTH, D_MODEL, N_MOD * D_MODEL), 0.3 * D_MODEL ** -0.5),
        "b_ada": nrm(k[5], (DEPTH, N_MOD * D_MODEL), 0.02),
        "norm1": gain(k[6], (DEPTH, D_MODEL)),
        "w_in": nrm(k[7], (DEPTH, D_MODEL, IN_PROJ_W), D_MODEL ** -0.5),
        "ret_decay_fwd": ret_logit[None, :] + nrm(k[8], (DEPTH, RET_HEADS), 0.1),
        "ret_decay_bwd": ret_logit[None, :] + nrm(k[9], (DEPTH, RET_HEADS), 0.1),
        "ret_norm": gain(k[10], (DEPTH, RET_W)),
        "na_q_norm": gain(k[11], (DEPTH, HEAD_DIM)),
        "na_k_norm": gain(k[12], (DEPTH, HEAD_DIM)),
        "na_rpb": nrm(k[13], (DEPTH, NA_HEADS, 2 * NA_WIN_R - 1, 2 * NA_WIN_C - 1), 0.2),
        "na_out_norm": gain(k[14], (DEPTH, NA_W)),
        "ga_q_norm": gain(k[15], (DEPTH, HEAD_DIM)),
        "ga_k_norm": gain(k[16], (DEPTH, HEAD_DIM)),
        "ga_out_norm": gain(k[17], (DEPTH, GA_W)),
        "w_out": nrm(k[18], (DEPTH, MIX_W, D_MODEL), MIX_W ** -0.5),
        "norm2": gain(k[19], (DEPTH, D_MODEL)),
        "w_router": nrm(k[20], (DEPTH, D_MODEL, N_EXPERTS), D_MODEL ** -0.5),
        "router_bias": nrm(k[21], (DEPTH, N_EXPERTS), 0.01),
        "w_exp_gate": nrm(k[22], (DEPTH, N_EXPERTS, D_MODEL, EXPERT_FF), D_MODEL ** -0.5),
        "w_exp_up": nrm(k[23], (DEPTH, N_EXPERTS, D_MODEL, EXPERT_FF), D_MODEL ** -0.5),
        "w_exp_down": nrm(k[24], (DEPTH, N_EXPERTS, EXPERT_FF, D_MODEL), EXPERT_FF ** -0.5),
        "w_sh_gate": nrm(k[25], (DEPTH, D_MODEL, SHARED_FF), D_MODEL ** -0.5),
        "w_sh_up": nrm(k[26], (DEPTH, D_MODEL, SHARED_FF), D_MODEL ** -0.5),
        "w_sh_down": nrm(k[27], (DEPTH, SHARED_FF, D_MODEL), SHARED_FF ** -0.5),
    }


def reference(x_prompt, x_sample, c_prompt, c_sample, w_ada, b_ada, norm1, w_in, ret_decay_fwd, ret_decay_bwd,
              ret_norm, na_q_norm, na_k_norm, na_rpb, na_out_norm, ga_q_norm, ga_k_norm, ga_out_norm, w_out,
              norm2, w_router, router_bias, w_exp_gate, w_exp_up, w_exp_down, w_sh_gate, w_sh_up, w_sh_down):
    layers = [
        (w_ada[l], b_ada[l], norm1[l], w_in[l], ret_decay_fwd[l], ret_decay_bwd[l], ret_norm[l],
         na_q_norm[l], na_k_norm[l], na_rpb[l], na_out_norm[l], ga_q_norm[l], ga_k_norm[l], ga_out_norm[l],
         w_out[l], norm2[l], w_router[l], router_bias[l], w_exp_gate[l], w_exp_up[l], w_exp_down[l],
         w_sh_gate[l], w_sh_up[l], w_sh_down[l])
        for l in range(DEPTH)]
    y_prompt = encoder_trunk(x_prompt, c_prompt, layers)
    y_sample = encoder_trunk(x_sample, c_sample, layers)
    return (y_prompt, y_sample)
```

```python
import dataclasses
import functools

import numpy as np
import jax
import jax.numpy as jnp
from jax import lax
from jax.experimental import pallas as pl
from jax.experimental.pallas import tpu as pltpu

F32 = jnp.float32
BF16 = jnp.bfloat16
I32 = jnp.int32

D_MODEL = 2048
HEAD_DIM = 128
RET_W = 512
NA_W = 512
GA_W = 1024
GA_KV_W = 256
PROJ_W = 4 * RET_W + 3 * NA_W + GA_W + 2 * GA_KV_W
RET_CHUNK = 128
GRID_W = 64
NA_WIN_R = 8
NA_WIN_C = 16
ROPE_BASE = 10000.0
N_EXPERTS = 64
N_GROUPS = 8
GROUP_SIZE = N_EXPERTS // N_GROUPS
TOPK_GROUPS = 4
TOP_K = 8
EXPERT_FF = 512
ROUTED_SCALE = 2.5
N_MOD = 6
EPS = 1e-6
QK_SCALE = HEAD_DIM ** -0.5
NEG_INF = float("-inf")
MASK_VALUE = -1e30

VMEM_LIMIT_BYTES = 56 * 1024 * 1024

PROJ_TN = 512
PROJ_TM = 512
NA_QROWS = 8
NA_SLAB_ROWS = 16
NA_KBLK_ROWS = 4
FLASH_TQ = 1024
FLASH_TK = 1024
OUT_TM = 256
ROUTER_TM = 512
DISPATCH_TM = 256
FFN_TM = 256
COMBINE_TM = 128
ADALN_TN = 1024


@dataclasses.dataclass(frozen=True)
class Geom:
    bp: int
    tp: int
    bs: int
    ts: int

    @property
    def n_p(self):
        return self.bp * self.tp

    @property
    def n(self):
        return self.bp * self.tp + self.bs * self.ts

    @property
    def nb(self):
        return self.bp + self.bs

    def locate(self, row):
        in_p = row < self.n_p
        rs = jnp.maximum(row - self.n_p, 0)
        b = jnp.where(in_p, row // self.tp, self.bp + rs // self.ts)
        pos = jnp.where(in_p, row % self.tp, rs % self.ts)
        return b, pos

    def seq_len(self, row):
        return jnp.where(row < self.n_p, self.tp, self.ts)


def _cparams(sem, vmem=VMEM_LIMIT_BYTES):
    return pltpu.CompilerParams(dimension_semantics=sem, vmem_limit_bytes=vmem)


def _dot(a, b):
    return jnp.dot(a, b, preferred_element_type=F32)


def _dot_nt(a, b):
    return lax.dot_general(a, b, (((1,), (1,)), ((), ())), preferred_element_type=F32)


def _sigmoid(x):
    return 1.0 / (1.0 + jnp.exp(-x))


def _rms(x, g):
    return x * lax.rsqrt(jnp.mean(x * x, axis=-1, keepdims=True) + EPS) * g


def _adaln_kernel(c_ref, w_ref, b_ref, o_ref):
    c = c_ref[...]
    s = c * _sigmoid(c)
    o_ref[...] = jnp.dot(s, w_ref[...], preferred_element_type=F32,
                         precision=lax.Precision.HIGHEST) + b_ref[...]


def _adaln(c_pad, w_ada, b_ada):
    depth, d, w = w_ada.shape
    rows = c_pad.shape[0]
    return pl.pallas_call(
        _adaln_kernel,
        out_shape=jax.ShapeDtypeStruct((depth, rows, w), F32),
        grid=(depth, w // ADALN_TN),
        in_specs=[
            pl.BlockSpec((rows, d), lambda l, j: (0, 0)),
            pl.BlockSpec((None, d, ADALN_TN), lambda l, j: (l, 0, j)),
            pl.BlockSpec((None, 1, ADALN_TN), lambda l, j: (l, 0, j)),
        ],
        out_specs=pl.BlockSpec((None, rows, ADALN_TN), lambda l, j: (l, 0, j)),
        compiler_params=_cparams(("parallel", "parallel")),
        name="adaln",
    )(c_pad, w_ada, b_ada.reshape(depth, 1, w))


def _rope_ret(x, cos, sin):
    return x * cos + pltpu.roll(x, HEAD_DIM // 2, 1) * sin


def _rope_axial(x, cos, sin):
    lane = lax.broadcasted_iota(I32, x.shape, 1)
    first = (lane & (HEAD_DIM // 4)) == 0
    rot = jnp.where(first, pltpu.roll(x, HEAD_DIM - HEAD_DIM // 4, 1), pltpu.roll(x, HEAD_DIM // 4, 1))
    return x * cos + rot * sin


def _inproj_kernel(x_ref, sc_ref, sh_ref, g1_ref, w_ref, rc_ref, rs_ref, ac_ref, as_ref, gains_ref,
                   o_ref, h_scr, acc_scr):
    j = pl.program_id(1)

    @pl.when(j == 0)
    def _():
        h = _rms(x_ref[...], g1_ref[...]) * (1.0 + sc_ref[...]) + sh_ref[...]
        h_scr[...] = h.astype(BF16)

    acc_scr[...] = _dot(h_scr[...], w_ref[...])
    heads = PROJ_TN // HEAD_DIM

    def head(hh):
        return acc_scr[:, hh * HEAD_DIM:(hh + 1) * HEAD_DIM]

    def put(hh, val):
        o_ref[:, hh * HEAD_DIM:(hh + 1) * HEAD_DIM] = val.astype(BF16)

    @pl.when(j == 0)
    def _():
        for hh in range(heads):
            put(hh, _rope_ret(head(hh), rc_ref[...], rs_ref[...]))

    @pl.when(j == 1)
    def _():
        for hh in range(heads):
            put(hh, _rope_ret(head(hh), rc_ref[...], rs_ref[...]) * QK_SCALE)

    @pl.when((j == 2) | (j == 6))
    def _():
        o_ref[...] = acc_scr[...].astype(BF16)

    @pl.when(j == 3)
    def _():
        a = acc_scr[...]
        o_ref[...] = (a * _sigmoid(a)).astype(BF16)

    @pl.when(j == 4)
    def _():
        for hh in range(heads):
            put(hh, _rms(head(hh), gains_ref[0:1, :]))

    @pl.when(j == 5)
    def _():
        for hh in range(heads):
            put(hh, _rms(head(hh), gains_ref[1:2, :]))

    @pl.when((j == 7) | (j == 8))
    def _():
        for hh in range(heads):
            put(hh, _rope_axial(_rms(head(hh), gains_ref[2:3, :]), ac_ref[...], as_ref[...]) * QK_SCALE)

    @pl.when(j == 9)
    def _():
        for hh in range(2):
            put(hh, _rope_axial(_rms(head(hh), gains_ref[3:4, :]), ac_ref[...], as_ref[...]))
        o_ref[:, 2 * HEAD_DIM:] = acc_scr[:, 2 * HEAD_DIM:].astype(BF16)


def _inproj(x, mod5, layer, g1, w_in_bf, tabs, gains, geo):
    n = x.shape[0]
    tm = PROJ_TM

    def mod_map(chunk):
        def f(i, j):
            b, _ = geo.locate(i * tm)
            return (layer, chunk, b, 0, 0)
        return f

    def tab_map(i, j):
        _, pos = geo.locate(i * tm)
        return (pos // tm, 0)

    mod_spec = lambda chunk: pl.BlockSpec((None, None, None, 1, D_MODEL), mod_map(chunk))
    tab_spec = pl.BlockSpec((tm, HEAD_DIM), tab_map)
    return pl.pallas_call(
        _inproj_kernel,
        out_shape=jax.ShapeDtypeStruct((n, PROJ_W), BF16),
        grid=(n // tm, PROJ_W // PROJ_TN),
        in_specs=[
            pl.BlockSpec((tm, D_MODEL), lambda i, j: (i, 0)),
            mod_spec(1), mod_spec(0),
            pl.BlockSpec((1, D_MODEL), lambda i, j: (0, 0)),
            pl.BlockSpec((D_MODEL, PROJ_TN), lambda i, j: (0, j)),
            tab_spec, tab_spec, tab_spec, tab_spec,
            pl.BlockSpec((4, HEAD_DIM), lambda i, j: (0, 0)),
        ],
        out_specs=pl.BlockSpec((tm, PROJ_TN), lambda i, j: (i, j)),
        scratch_shapes=[pltpu.VMEM((tm, D_MODEL), BF16), pltpu.VMEM((tm, PROJ_TN), F32)],
        compiler_params=_cparams(("parallel", "arbitrary")),
        name="inproj",
    )(x, mod5, mod5, g1, w_in_bf, tabs[0], tabs[1], tabs[2], tabs[3], gains)


def _ret_kernel(lg_ref, q_ref, k_ref, v_ref, *rest, reverse, geo):
    if reverse:
        o_ref, s_scr, tab_scr = rest
    else:
        sg_ref, ob_ref, gn_ref, o_ref, s_scr, tab_scr = rest
    i = pl.program_id(0)
    nchunks = pl.num_programs(0)
    c = RET_CHUNK
    heads = RET_W // HEAD_DIM

    @pl.when(i == 0)
    def _():
        row = lax.broadcasted_iota(I32, (c, c), 0).astype(F32)
        col = lax.broadcasted_iota(I32, (c, c), 1).astype(F32)
        for hh in range(heads):
            lg = lg_ref[hh]
            if reverse:
                diff = col - row
                decay = jnp.where(diff > 0, jnp.exp(lg * jnp.maximum(diff, 0.0)), 0.0)
                qdec = jnp.exp(lg * (c - row))
                kdec = jnp.exp(lg * row)
            else:
                diff = row - col
                decay = jnp.where(diff >= 0, jnp.exp(lg * jnp.maximum(diff, 0.0)), 0.0)
                qdec = jnp.exp(lg * (row + 1.0))
                kdec = jnp.exp(lg * (c - 1.0 - row))
            tab_scr[hh, 0] = decay
            tab_scr[hh, 1] = qdec
            tab_scr[hh, 2] = kdec
            tab_scr[hh, 3] = jnp.exp(jnp.zeros((c, c), F32) + lg * c)

    chunk = (nchunks - 1 - i) if reverse else i
    row0 = chunk * c
    _, pos = geo.locate(row0)
    boundary = (pos + c == geo.seq_len(row0)) if reverse else (pos == 0)

    @pl.when(boundary)
    def _():
        s_scr[...] = jnp.zeros_like(s_scr)

    for hh in range(heads):
        sl = slice(hh * HEAD_DIM, (hh + 1) * HEAD_DIM)
        qh = q_ref[:, sl]
        kh = k_ref[:, sl]
        vh = v_ref[:, sl]
        s = _dot_nt(qh, kh) * tab_scr[hh, 0]
        o = _dot(s.astype(BF16), vh)
        qd = (qh.astype(F32) * tab_scr[hh, 1]).astype(BF16)
        o = o + _dot(qd, s_scr[hh].astype(BF16))
        kd_t = (kh.astype(F32) * tab_scr[hh, 2]).T.astype(BF16)
        s_scr[hh] = s_scr[hh] * tab_scr[hh, 3] + _dot(kd_t, vh)
        if reverse:
            o_ref[:, sl] = o
        else:
            tot = o + ob_ref[:, sl]
            y = _rms(tot, gn_ref[:, sl])
            o_ref[:, sl] = (sg_ref[:, sl].astype(F32) * y).astype(BF16)


def _retention(proj, lg_f, lg_b, ret_gn, geo):
    n = proj.shape[0]
    nchunks = n // RET_CHUNK
    c = RET_CHUNK
    heads = RET_W // HEAD_DIM
    scratch = [pltpu.VMEM((heads, HEAD_DIM, HEAD_DIM), F32), pltpu.VMEM((heads, 4, c, c), F32)]
    smem = pl.BlockSpec(memory_space=pltpu.SMEM)

    def col_spec(colblk, rev):
        if rev:
            return pl.BlockSpec((c, RET_W), lambda i: (nchunks - 1 - i, colblk))
        return pl.BlockSpec((c, RET_W), lambda i: (i, colblk))

    o_bwd = pl.pallas_call(
        functools.partial(_ret_kernel, reverse=True, geo=geo),
        out_shape=jax.ShapeDtypeStruct((n, RET_W), F32),
        grid=(nchunks,),
        in_specs=[smem, col_spec(0, True), col_spec(1, True), col_spec(2, True)],
        out_specs=pl.BlockSpec((c, RET_W), lambda i: (nchunks - 1 - i, 0)),
        scratch_shapes=scratch,
        compiler_params=_cparams(("arbitrary",)),
        name="ret_bwd",
    )(lg_b, proj, proj, proj)
    return pl.pallas_call(
        functools.partial(_ret_kernel, reverse=False, geo=geo),
        out_shape=jax.ShapeDtypeStruct((n, RET_W), BF16),
        grid=(nchunks,),
        in_specs=[smem, col_spec(0, False), col_spec(1, False), col_spec(2, False), col_spec(3, False),
                  pl.BlockSpec((c, RET_W), lambda i: (i, 0)),
                  pl.BlockSpec((1, RET_W), lambda i: (0, 0))],
        out_specs=pl.BlockSpec((c, RET_W), lambda i: (i, 0)),
        scratch_shapes=scratch,
        compiler_params=_cparams(("arbitrary",)),
        name="ret_fwd",
    )(lg_f, proj, proj, proj, proj, o_bwd, ret_gn)


def _na_geometry(i, geo):
    rp, rs = geo.tp // GRID_W, geo.ts // GRID_W
    r0g = i * NA_QROWS
    in_p = r0g < geo.bp * rp
    rsmp = jnp.maximum(r0g - geo.bp * rp, 0)
    rows = jnp.where(in_p, rp, rs)
    lr0 = jnp.where(in_p, r0g % rp, rsmp % rs)
    slab = jnp.clip(lr0 - NA_WIN_R // 2, 0, rows - NA_SLAB_ROWS)
    return rows, lr0, slab, r0g - lr0


def _na_kernel(q_ref, k0, k1, k2, k3, v0, v1, v2, v3, tab_ref, on_ref, o_ref, ks_scr, vs_scr, *, geo):
    i = pl.program_id(0)
    rows, lr0, slab, _ = _na_geometry(i, geo)
    blk = NA_KBLK_ROWS * GRID_W
    for m, (kr, vr) in enumerate(((k0, v0), (k1, v1), (k2, v2), (k3, v3))):
        ks_scr[m * blk:(m + 1) * blk, :] = kr[...]
        vs_scr[m * blk:(m + 1) * blk, :] = vr[...]
    heads = NA_W // HEAD_DIM
    win = NA_WIN_R * GRID_W

    def body(p, carry):
        r = lr0 + p
        rs = jnp.clip(r - NA_WIN_R // 2, 0, rows - NA_WIN_R)
        off = pl.multiple_of((rs - slab) * GRID_W, GRID_W)
        var = r - rs
        qrow = q_ref[pl.ds(pl.multiple_of(p * GRID_W, GRID_W), GRID_W), :]
        outs = []
        for hh in range(heads):
            sl = slice(hh * HEAD_DIM, (hh + 1) * HEAD_DIM)
            kw = ks_scr[pl.ds(off, win), sl]
            vw = vs_scr[pl.ds(off, win), sl]
            s = _dot_nt(qrow[:, sl], kw) * QK_SCALE + tab_ref[hh, var]
            e = jnp.exp(s - jnp.max(s, axis=-1, keepdims=True))
            prob = e / jnp.sum(e, axis=-1, keepdims=True)
            outs.append(_dot(prob.astype(BF16), vw))
        o = jnp.concatenate(outs, axis=-1)
        o_ref[pl.ds(pl.multiple_of(p * GRID_W, GRID_W), GRID_W), :] = _rms(o, on_ref[...]).astype(BF16)
        return carry

    lax.fori_loop(0, NA_QROWS, body, 0)


def _na_bias_table(rpb):
    v = np.arange(NA_WIN_R)[:, None, None, None]
    c = np.arange(GRID_W)[None, :, None, None]
    i = np.arange(NA_WIN_R)[None, None, :, None]
    j = np.arange(GRID_W)[None, None, None, :]
    cs = np.clip(c - NA_WIN_C // 2, 0, GRID_W - NA_WIN_C)
    valid = np.broadcast_to((j >= cs) & (j < cs + NA_WIN_C), (NA_WIN_R, GRID_W, NA_WIN_R, GRID_W))
    dr = np.broadcast_to(i - v + (NA_WIN_R - 1), valid.shape)
    dc = np.broadcast_to(np.clip(j - c + (NA_WIN_C - 1), 0, 2 * NA_WIN_C - 2), valid.shape)
    vals = rpb.astype(F32)[:, dr, dc]
    tab = jnp.where(valid[None], vals, MASK_VALUE)
    return tab.reshape(rpb.shape[0], NA_WIN_R, GRID_W, NA_WIN_R * GRID_W)


def _neighbourhood(proj, bias_tab, na_on, geo):
    n = proj.shape[0]
    tq = NA_QROWS * GRID_W
    blk = NA_KBLK_ROWS * GRID_W
    nblk = NA_SLAB_ROWS // NA_KBLK_ROWS
    qcol, kcol, vcol = 4, 5, 6

    def kv_spec(colblk, m):
        def f(i):
            _, _, slab, seq_row0 = _na_geometry(i, geo)
            return ((seq_row0 + slab) // NA_KBLK_ROWS + m, colblk)
        return pl.BlockSpec((blk, NA_W), f)

    heads = NA_W // HEAD_DIM
    return pl.pallas_call(
        functools.partial(_na_kernel, geo=geo),
        out_shape=jax.ShapeDtypeStruct((n, NA_W), BF16),
        grid=(n // tq,),
        in_specs=[pl.BlockSpec((tq, NA_W), lambda i: (i, qcol))]
        + [kv_spec(kcol, m) for m in range(nblk)]
        + [kv_spec(vcol, m) for m in range(nblk)]
        + [pl.BlockSpec((heads, NA_WIN_R, GRID_W, NA_WIN_R * GRID_W), lambda i: (0, 0, 0, 0)),
           pl.BlockSpec((1, NA_W), lambda i: (0, 0))],
        out_specs=pl.BlockSpec((tq, NA_W), lambda i: (i, 0)),
        scratch_shapes=[pltpu.VMEM((NA_SLAB_ROWS * GRID_W, NA_W), BF16),
                        pltpu.VMEM((NA_SLAB_ROWS * GRID_W, NA_W), BF16)],
        compiler_params=_cparams(("parallel",)),
        name="natten",
    )(*([proj] * (1 + 2 * nblk)), bias_tab, na_on)


def _flash_kernel(q_ref, k_ref, v_ref, o_ref, m_scr, l_scr, acc_scr):
    ki = pl.program_id(3)
    groups = q_ref.shape[1] // HEAD_DIM

    @pl.when(ki == 0)
    def _():
        m_scr[...] = jnp.full_like(m_scr, NEG_INF)
        l_scr[...] = jnp.zeros_like(l_scr)
        acc_scr[...] = jnp.zeros_like(acc_scr)

    k = k_ref[...]
    v = v_ref[...]
    for gq in range(groups):
        q = q_ref[:, gq * HEAD_DIM:(gq + 1) * HEAD_DIM]
        s = _dot_nt(q, k)
        m_prev = m_scr[gq][:, :1]
        m_next = jnp.maximum(m_prev, jnp.max(s, axis=1, keepdims=True))
        alpha = jnp.exp(m_prev - m_next)
        p = jnp.exp(s - m_next)
        l_scr[gq] = alpha * l_scr[gq] + jnp.sum(p, axis=1, keepdims=True)
        acc_scr[gq] = alpha * acc_scr[gq] + _dot(p.astype(BF16), v)
        m_scr[gq] = jnp.broadcast_to(m_next, m_scr.shape[1:])

    @pl.when(ki == pl.num_programs(3) - 1)
    def _():
        for gq in range(groups):
            o_ref[:, gq * HEAD_DIM:(gq + 1) * HEAD_DIM] = (acc_scr[gq] / l_scr[gq]).astype(BF16)


def _flash_group(proj, row0, batch, t):
    tq, tk = min(FLASH_TQ, t), min(FLASH_TK, t)
    kv_heads = GA_KV_W // HEAD_DIM
    qw = GA_W // kv_heads
    groups = qw // HEAD_DIM
    qcol0 = (4 * RET_W + 3 * NA_W) // qw
    kcol0 = (4 * RET_W + 3 * NA_W + GA_W) // HEAD_DIM
    vcol0 = kcol0 + kv_heads
    return pl.pallas_call(
        _flash_kernel,
        out_shape=jax.ShapeDtypeStruct((batch * t, GA_W), BF16),
        grid=(batch, kv_heads, t // tq, t // tk),
        in_specs=[
            pl.BlockSpec((tq, qw), lambda b, h, qi, ki: ((row0 + b * t) // tq + qi, qcol0 + h)),
            pl.BlockSpec((tk, HEAD_DIM), lambda b, h, qi, ki: ((row0 + b * t) // tk + ki, kcol0 + h)),
            pl.BlockSpec((tk, HEAD_DIM), lambda b, h, qi, ki: ((row0 + b * t) // tk + ki, vcol0 + h)),
        ],
        out_specs=pl.BlockSpec((tq, qw), lambda b, h, qi, ki: ((b * t) // tq + qi, h)),
        scratch_shapes=[pltpu.VMEM((groups, tq, HEAD_DIM), F32),
                        pltpu.VMEM((groups, tq, HEAD_DIM), F32),
                        pltpu.VMEM((groups, tq, HEAD_DIM), F32)],
        compiler_params=_cparams(("parallel", "parallel", "parallel", "arbitrary")),
        name="flash_gqa",
    )(proj, proj, proj)


def _outproj_kernel(yr_ref, yn_ref, yg_ref, x_ref, w_ref, gon_ref, gt_ref, sc_ref, sh_ref, g2_ref,
                    x1_ref, h2_ref):
    ygn = _rms(yg_ref[...].astype(F32), gon_ref[...]).astype(BF16)
    acc = _dot(yr_ref[...], w_ref[0:RET_W, :])
    acc = acc + _dot(yn_ref[...], w_ref[RET_W:RET_W + NA_W, :])
    acc = acc + _dot(ygn, w_ref[RET_W + NA_W:, :])
    x1 = x_ref[...] + gt_ref[...] * acc
    x1_ref[...] = x1
    h2_ref[...] = _rms(x1, g2_ref[...]) * (1.0 + sc_ref[...]) + sh_ref[...]


def _outproj(y_ret, y_na, y_ga, x, w_out_bf, ga_on, mod5, layer, g2, geo):
    n = x.shape[0]
    tm = OUT_TM

    def mod_spec(chunk):
        def f(i):
            b, _ = geo.locate(i * tm)
            return (layer, chunk, b, 0, 0)
        return pl.BlockSpec((None, None, None, 1, D_MODEL), f)

    row = lambda w: pl.BlockSpec((tm, w), lambda i: (i, 0))
    return pl.pallas_call(
        _outproj_kernel,
        out_shape=(jax.ShapeDtypeStruct((n, D_MODEL), F32), jax.ShapeDtypeStruct((n, D_MODEL), F32)),
        grid=(n // tm,),
        in_specs=[row(RET_W), row(NA_W), row(GA_W), row(D_MODEL),
                  pl.BlockSpec((D_MODEL, D_MODEL), lambda i: (0, 0)),
                  pl.BlockSpec((1, GA_W), lambda i: (0, 0)),
                  mod_spec(2), mod_spec(4), mod_spec(3),
                  pl.BlockSpec((1, D_MODEL), lambda i: (0, 0))],
        out_specs=(row(D_MODEL), row(D_MODEL)),
        compiler_params=_cparams(("parallel",)),
        name="outproj",
    )(y_ret, y_na, y_ga, x, w_out_bf, ga_on, mod5, mod5, mod5, g2)


def _first_index_of_max(vals, ids, axes, sentinel):
    m = vals
    for ax in axes:
        m = jnp.max(m, axis=ax, keepdims=True)
    cand = jnp.where(vals == m, ids, sentinel)
    for ax in axes:
        cand = jnp.min(cand, axis=ax, keepdims=True)
    return m, cand


def _router_kernel(h_ref, w_ref, b_ref, idx_ref, wgt_ref, rank_ref, cnt_ref, cnt_scr, tri_scr):
    i = pl.program_id(0)
    tm = h_ref.shape[0]

    @pl.when(i == 0)
    def _():
        cnt_scr[...] = jnp.zeros_like(cnt_scr)
        r = lax.broadcasted_iota(I32, (tm, tm), 0)
        c = lax.broadcasted_iota(I32, (tm, tm), 1)
        tri_scr[...] = jnp.where(r < c, 1.0, 0.0).astype(BF16)

    logits = lax.dot_general(w_ref[...], h_ref[...], (((1,), (1,)), ((), ())),
                             preferred_element_type=F32, precision=lax.Precision.HIGHEST)
    scores = _sigmoid(logits)
    choice = scores + b_ref[...][:, :1]
    shape3 = (N_GROUPS, GROUP_SIZE, tm)
    choice3 = choice.reshape(shape3)
    scores3 = scores.reshape(shape3)
    sub = lax.broadcasted_iota(I32, shape3, 1)
    grp = lax.broadcasted_iota(I32, shape3, 0)
    eid = grp * GROUP_SIZE + sub

    m1, i1 = _first_index_of_max(choice3, sub, (1,), GROUP_SIZE)
    rest = jnp.where(sub == i1, NEG_INF, choice3)
    m2 = jnp.max(rest, axis=1, keepdims=True)
    gscore = m1 + m2

    gid = lax.broadcasted_iota(I32, gscore.shape, 0)
    gsel = jnp.zeros(gscore.shape, F32)
    for _ in range(TOPK_GROUPS):
        _, gi = _first_index_of_max(gscore, gid, (0,), N_GROUPS)
        hit = gid == gi
        gsel = jnp.where(hit, 1.0, gsel)
        gscore = jnp.where(hit, NEG_INF, gscore)

    masked = jnp.where(gsel > 0.0, choice3, NEG_INF)
    onehots, ids, ws = [], [], []
    for _ in range(TOP_K):
        _, ei = _first_index_of_max(masked, eid, (1, 0), N_EXPERTS)
        hit = eid == ei
        onehots.append(hit)
        ids.append(ei.reshape(1, tm))
        ws.append(jnp.sum(jnp.where(hit, scores3, 0.0), axis=(0, 1), keepdims=True).reshape(1, tm))
        masked = jnp.where(hit, NEG_INF, masked)
    wsum = ws[0]
    for k in range(1, TOP_K):
        wsum = wsum + ws[k]

    sel = jnp.zeros(shape3, F32)
    for hit in onehots:
        sel = jnp.where(hit, 1.0, sel)
    sel2 = sel.reshape(N_EXPERTS, tm)
    before = _dot(sel2.astype(BF16), tri_scr[...]) + cnt_scr[:, :1]
    before3 = before.reshape(shape3)
    ranks = [jnp.sum(jnp.where(hit, before3, 0.0), axis=(0, 1), keepdims=True).reshape(1, tm)
             for hit in onehots]
    cnt_scr[...] = cnt_scr[...] + jnp.sum(sel2, axis=1, keepdims=True)

    idx_ref[...] = jnp.concatenate(ids, axis=0)
    wgt_ref[...] = jnp.concatenate([w / wsum * ROUTED_SCALE for w in ws], axis=0)
    rank_ref[...] = jnp.concatenate(ranks, axis=0).astype(I32)
    cnt_ref[...] = cnt_scr[...]


def _router(h2, w_router_t, bias_col):
    n = h2.shape[0]
    tm = ROUTER_TM
    out_blk = pl.BlockSpec((TOP_K, tm), lambda i: (0, i))
    return pl.pallas_call(
        _router_kernel,
        out_shape=(jax.ShapeDtypeStruct((TOP_K, n), I32), jax.ShapeDtypeStruct((TOP_K, n), F32),
                   jax.ShapeDtypeStruct((TOP_K, n), I32), jax.ShapeDtypeStruct((N_EXPERTS, HEAD_DIM), F32)),
        grid=(n // tm,),
        in_specs=[pl.BlockSpec((tm, D_MODEL), lambda i: (i, 0)),
                  pl.BlockSpec((N_EXPERTS, D_MODEL), lambda i: (0, 0)),
                  pl.BlockSpec((N_EXPERTS, HEAD_DIM), lambda i: (0, 0))],
        out_specs=(out_blk, out_blk, out_blk, pl.BlockSpec((N_EXPERTS, HEAD_DIM), lambda i: (0, 0))),
        scratch_shapes=[pltpu.VMEM((N_EXPERTS, HEAD_DIM), F32), pltpu.VMEM((tm, tm), BF16)],
        compiler_params=_cparams(("arbitrary",)),
        name="router",
    )(h2, w_router_t, bias_col)


def _row_copy(src_hbm, src_row, dst_hbm, dst_row, sem, rows=1):
    return pltpu.make_async_copy(src_hbm.at[pl.ds(src_row, rows)], dst_hbm.at[pl.ds(dst_row, rows)], sem)


def _zero_fill_padding(fill_ref, end_ref, xs_hbm, zero_scr, zsem, wait):
    def go(copy):
        copy.wait() if wait else copy.start()

    def per_expert(e, carry):
        def per_row(r, c):
            go(pltpu.make_async_copy(zero_scr.at[pl.ds(0, 1)], xs_hbm.at[pl.ds(r, 1)], zsem))
            return c
        return lax.fori_loop(fill_ref[e], end_ref[e], per_row, carry)

    lax.fori_loop(0, N_EXPERTS, per_expert, 0)
    total = end_ref[N_EXPERTS - 1]
    n_tiles = xs_hbm.shape[0] // FFN_TM
    for tile in range(n_tiles - N_EXPERTS, n_tiles):
        @pl.when(tile * FFN_TM >= total)
        def _(tile=tile):
            go(pltpu.make_async_copy(zero_scr, xs_hbm.at[pl.ds(tile * FFN_TM, FFN_TM)], zsem))


def _dispatch_kernel(seg_ref, fill_ref, end_ref, idx_ref, rank_ref, h_hbm, xs_hbm, zero_scr, sem, zsem):
    i = pl.program_id(0)
    tm = idx_ref.shape[1]

    @pl.when(i == 0)
    def _():
        zero_scr[...] = jnp.zeros_like(zero_scr)
        _zero_fill_padding(fill_ref, end_ref, xs_hbm, zero_scr, zsem, wait=False)
        _zero_fill_padding(fill_ref, end_ref, xs_hbm, zero_scr, zsem, wait=True)

    def issue(t, carry):
        for k in range(TOP_K):
            dst = seg_ref[idx_ref[k, t]] + rank_ref[k, t]
            _row_copy(h_hbm, i * tm + t, xs_hbm, dst, sem).start()
        return carry

    lax.fori_loop(0, tm, issue, 0)
    def drain(t, carry):
        _row_copy(h_hbm, 0, xs_hbm, 0, sem, rows=TOP_K).wait()
        return carry

    lax.fori_loop(0, tm, drain, 0)


def _dispatch(h2, idx, rank, seg_start, seg_fill, seg_end, n_slots):
    n = h2.shape[0]
    tm = DISPATCH_TM
    smem_blk = pl.BlockSpec((TOP_K, tm), lambda i, *_: (0, i), memory_space=pltpu.SMEM)
    return pl.pallas_call(
        _dispatch_kernel,
        out_shape=jax.ShapeDtypeStruct((n_slots, D_MODEL), F32),
        grid_spec=pltpu.PrefetchScalarGridSpec(
            num_scalar_prefetch=3,
            grid=(n // tm,),
            in_specs=[smem_blk, smem_blk, pl.BlockSpec(memory_space=pl.ANY)],
            out_specs=pl.BlockSpec(memory_space=pl.ANY),
            scratch_shapes=[pltpu.VMEM((FFN_TM, D_MODEL), F32),
                            pltpu.SemaphoreType.DMA(()), pltpu.SemaphoreType.DMA(())],
        ),
        compiler_params=_cparams(("arbitrary",)),
        name="moe_dispatch",
    )(seg_start, seg_fill, seg_end, idx, rank, h2)


def _ffn_kernel(te_ref, nv_ref, x_ref, wg_ref, wu_ref, wd_ref, o_ref):
    i = pl.program_id(0)
    nvalid = nv_ref[i]

    @pl.when(nvalid > 0)
    def _():
        x = x_ref[...].astype(BF16)
        hg = _dot(x, wg_ref[...])
        hu = _dot(x, wu_ref[...])
        h = (hg * _sigmoid(hg) * hu).astype(BF16)
        o_ref[...] = _dot(h, wd_ref[...])

    @pl.when(nvalid == 0)
    def _():
        o_ref[...] = jnp.zeros_like(o_ref)


def _expert_ffn(x_sorted, tile_expert, tile_valid, wg, wu, wd):
    n_slots = x_sorted.shape[0]
    tm = FFN_TM
    return pl.pallas_call(
        _ffn_kernel,
        out_shape=jax.ShapeDtypeStruct((n_slots, D_MODEL), F32),
        grid_spec=pltpu.PrefetchScalarGridSpec(
            num_scalar_prefetch=2,
            grid=(n_slots // tm,),
            in_specs=[pl.BlockSpec((tm, D_MODEL), lambda i, te, nv: (i, 0)),
                      pl.BlockSpec((None, D_MODEL, EXPERT_FF), lambda i, te, nv: (te[i], 0, 0)),
                      pl.BlockSpec((None, D_MODEL, EXPERT_FF), lambda i, te, nv: (te[i], 0, 0)),
                      pl.BlockSpec((None, EXPERT_FF, D_MODEL), lambda i, te, nv: (te[i], 0, 0))],
            out_specs=pl.BlockSpec((tm, D_MODEL), lambda i, te, nv: (i, 0)),
        ),
        compiler_params=_cparams(("parallel",)),
        name="moe_ffn",
    )(tile_expert, tile_valid, x_sorted, wg, wu, wd)


def _combine_kernel(seg_ref, idx_ref, rank_ref, idxn_ref, rankn_ref, wgt_ref, x1_ref, h_ref, gt_ref,
                    wsg_ref, wsu_ref, wsd_ref, y_hbm, o_ref, ybuf, sems):
    i = pl.program_id(0)
    nsteps = pl.num_programs(0)
    tm = x1_ref.shape[0]

    def issue(slot, ir, rr):
        def body(t, carry):
            for k in range(TOP_K):
                src = seg_ref[ir[k, t]] + rr[k, t]
                pltpu.make_async_copy(y_hbm.at[pl.ds(src, 1)], ybuf.at[slot, k, pl.ds(t, 1)],
                                      sems.at[slot]).start()
            return carry
        lax.fori_loop(0, tm, body, 0)

    slot = i % 2

    @pl.when(i == 0)
    def _():
        issue(0, idx_ref, rank_ref)

    @pl.when(i + 1 < nsteps)
    def _():
        issue(1 - slot, idxn_ref, rankn_ref)

    hb = h_ref[...].astype(BF16)
    hg = _dot(hb, wsg_ref[...])
    hu = _dot(hb, wsu_ref[...])
    shared = _dot((hg * _sigmoid(hg) * hu).astype(BF16), wsd_ref[...])

    for k in range(TOP_K):
        pltpu.make_async_copy(y_hbm.at[pl.ds(0, tm)], ybuf.at[slot, k], sems.at[slot]).wait()

    acc = shared
    for k in range(TOP_K):
        acc = acc + wgt_ref[:, k:k + 1] * ybuf[slot, k]
    o_ref[...] = x1_ref[...] + gt_ref[...] * acc


def _combine(y_sorted, idx, rank, wgt_t, seg_start, x1, h2, mod5, layer, wsg, wsu, wsd, geo):
    n = x1.shape[0]
    tm = COMBINE_TM
    nsteps = n // tm
    smem_cur = pl.BlockSpec((TOP_K, tm), lambda i, seg: (0, i), memory_space=pltpu.SMEM)
    smem_next = pl.BlockSpec((TOP_K, tm), lambda i, seg: (0, jnp.minimum(i + 1, nsteps - 1)),
                             memory_space=pltpu.SMEM)

    def gt_map(i, seg):
        b, _ = geo.locate(i * tm)
        return (layer, 5, b, 0, 0)

    row = pl.BlockSpec((tm, D_MODEL), lambda i, seg: (i, 0))
    whole = lambda a, b: pl.BlockSpec((a, b), lambda i, seg: (0, 0))
    return pl.pallas_call(
        _combine_kernel,
        out_shape=jax.ShapeDtypeStruct((n, D_MODEL), F32),
        grid_spec=pltpu.PrefetchScalarGridSpec(
            num_scalar_prefetch=1,
            grid=(nsteps,),
            in_specs=[smem_cur, smem_cur, smem_next, smem_next,
                      pl.BlockSpec((tm, TOP_K), lambda i, seg: (i, 0)),
                      row, row,
                      pl.BlockSpec((None, None, None, 1, D_MODEL), gt_map),
                      whole(D_MODEL, EXPERT_FF), whole(D_MODEL, EXPERT_FF), whole(EXPERT_FF, D_MODEL),
                      pl.BlockSpec(memory_space=pl.ANY)],
            out_specs=row,
            scratch_shapes=[pltpu.VMEM((2, TOP_K, tm, D_MODEL), F32), pltpu.SemaphoreType.DMA((2,))],
        ),
        compiler_params=_cparams(("arbitrary",)),
        name="moe_combine",
    )(seg_start, idx, rank, idx, rank, wgt_t, x1, h2, mod5, wsg, wsu, wsd, y_sorted)


def _rope_tables(t_max):
    pos = jnp.arange(t_max, dtype=jnp.int32)

    def angles(p, n_pairs):
        inv = ROPE_BASE ** (-jnp.arange(n_pairs, dtype=F32) / n_pairs)
        ang = p.astype(F32)[:, None] * inv[None, :]
        return jnp.cos(ang), jnp.sin(ang)

    c, s = angles(pos, HEAD_DIM // 2)
    cr, sr = angles(pos // GRID_W, HEAD_DIM // 4)
    cc, sc = angles(pos % GRID_W, HEAD_DIM // 4)
    return (jnp.concatenate([c, c], -1), jnp.concatenate([-s, s], -1),
            jnp.concatenate([cr, cr, cc, cc], -1), jnp.concatenate([-sr, sr, -sc, sc], -1))


def _moe_plan(counts, n_slots):
    padded = (counts + FFN_TM - 1) // FFN_TM * FFN_TM
    seg_end = jnp.cumsum(padded)
    seg_start = seg_end - padded
    tile_row0 = jnp.arange(n_slots // FFN_TM, dtype=I32) * FFN_TM
    tile_expert = jnp.minimum(jnp.searchsorted(seg_end, tile_row0, side="right"), N_EXPERTS - 1).astype(I32)
    tile_valid = jnp.clip(seg_start[tile_expert] + counts[tile_expert] - tile_row0, 0, FFN_TM).astype(I32)
    return (seg_start.astype(I32), (seg_start + counts).astype(I32), seg_end.astype(I32),
            tile_expert, tile_valid)


def _layer(x, mod5, layer, p, tabs, geo):
    n = x.shape[0]
    proj = _inproj(x, mod5, layer, p["g1"], p["w_in"], tabs, p["gains"], geo)
    y_ret = _retention(proj, p["lg_f"], p["lg_b"], p["ret_gn"], geo)
    y_na = _neighbourhood(proj, p["na_tab"], p["na_on"], geo)
    y_ga = jnp.concatenate([_flash_group(proj, 0, geo.bp, geo.tp),
                            _flash_group(proj, geo.n_p, geo.bs, geo.ts)], axis=0)
    x1, h2 = _outproj(y_ret, y_na, y_ga, x, p["w_out"], p["ga_on"], mod5, layer, p["g2"], geo)
    idx, wgt, rank, cnt = _router(h2, p["w_router_t"], p["router_bias"])
    n_slots = n * TOP_K + N_EXPERTS * FFN_TM
    seg_start, seg_fill, seg_end, tile_expert, tile_valid = _moe_plan(cnt[:, 0].astype(I32), n_slots)
    x_sorted = _dispatch(h2, idx, rank, seg_start, seg_fill, seg_end, n_slots)
    y_sorted = _expert_ffn(x_sorted, tile_expert, tile_valid, p["w_eg"], p["w_eu"], p["w_ed"])
    return _combine(y_sorted, idx, rank, wgt.T, seg_start, x1, h2, mod5, layer,
                    p["w_sg"], p["w_su"], p["w_sd"], geo)


def kernel(x_prompt, x_sample, c_prompt, c_sample, w_ada, b_ada, norm1, w_in, ret_decay_fwd, ret_decay_bwd, ret_norm, na_q_norm, na_k_norm, na_rpb, na_out_norm, ga_q_norm, ga_k_norm, ga_out_norm, w_out, norm2, w_router, router_bias, w_exp_gate, w_exp_up, w_exp_down, w_sh_gate, w_sh_up, w_sh_down):
    bp, tp, d = x_prompt.shape
    bs, ts, _ = x_sample.shape
    depth = w_ada.shape[0]
    geo = Geom(bp, tp, bs, ts)
    assert d == D_MODEL and w_in.shape[-1] == PROJ_W
    for t in (tp, ts):
        assert t % max(PROJ_TM, FLASH_TQ, FLASH_TK, NA_SLAB_ROWS * GRID_W) == 0

    x = jnp.concatenate([x_prompt.reshape(bp * tp, d), x_sample.reshape(bs * ts, d)], axis=0)
    c_all = jnp.concatenate([c_prompt, c_sample], axis=0)
    c_pad = jnp.zeros((8 * pl.cdiv(geo.nb, 8), d), F32).at[:geo.nb].set(c_all)
    mod = _adaln(c_pad, w_ada, b_ada)
    rows = mod.shape[1]
    mod5 = mod.reshape(depth, rows, N_MOD, 1, d).transpose(0, 2, 1, 3, 4)
    tabs = _rope_tables(max(tp, ts))

    for l in range(depth):
        p = {
            "g1": norm1[l].reshape(1, d),
            "w_in": w_in[l].astype(BF16),
            "gains": jnp.stack([na_q_norm[l], na_k_norm[l], ga_q_norm[l], ga_k_norm[l]]),
            "lg_f": jax.nn.log_sigmoid(ret_decay_fwd[l].astype(F32)),
            "lg_b": jax.nn.log_sigmoid(ret_decay_bwd[l].astype(F32)),
            "ret_gn": ret_norm[l].reshape(1, RET_W),
            "na_tab": _na_bias_table(na_rpb[l]),
            "na_on": na_out_norm[l].reshape(1, NA_W),
            "ga_on": ga_out_norm[l].reshape(1, GA_W),
            "w_out": w_out[l].astype(BF16),
            "g2": norm2[l].reshape(1, d),
            "w_router_t": w_router[l].astype(F32).T,
            "router_bias": jnp.broadcast_to(router_bias[l].astype(F32)[:, None], (N_EXPERTS, HEAD_DIM)),
            "w_eg": w_exp_gate[l].astype(BF16),
            "w_eu": w_exp_up[l].astype(BF16),
            "w_ed": w_exp_down[l].astype(BF16),
            "w_sg": w_sh_gate[l].astype(BF16),
            "w_su": w_sh_up[l].astype(BF16),
            "w_sd": w_sh_down[l].astype(BF16),
        }
        x = _layer(x, mod5, l, p, tabs, geo)

    y_prompt = x[:geo.n_p].reshape(bp, tp, d)
    y_sample = x[geo.n_p:].reshape(bs, ts, d)
    return (y_prompt, y_sample)
```

```python
import dataclasses
import functools

import numpy as np
import jax
import jax.numpy as jnp
from jax import lax
from jax.experimental import pallas as pl
from jax.experimental.pallas import tpu as pltpu

F32 = jnp.float32
BF16 = jnp.bfloat16
I32 = jnp.int32

D_MODEL = 2048
HEAD_DIM = 128
RET_W = 512
NA_W = 512
GA_W = 1024
GA_KV_W = 256
PROJ_W = 4 * RET_W + 3 * NA_W + GA_W + 2 * GA_KV_W
RET_CHUNK = 128
GRID_W = 64
NA_WIN_R = 8
NA_WIN_C = 16
ROPE_BASE = 10000.0
N_EXPERTS = 64
N_GROUPS = 8
GROUP_SIZE = N_EXPERTS // N_GROUPS
TOPK_GROUPS = 4
TOP_K = 8
EXPERT_FF = 512
ROUTED_SCALE = 2.5
N_MOD = 6
EPS = 1e-6
QK_SCALE = HEAD_DIM ** -0.5
NEG_INF = float("-inf")
MASK_VALUE = -1e30

VMEM_LIMIT_BYTES = 56 * 1024 * 1024

PROJ_TN = 512
PROJ_TM = 512
NA_QROWS = 8
NA_SLAB_ROWS = 16
NA_KBLK_ROWS = 4
FLASH_TQ = 1024
FLASH_TK = 1024
OUT_TM = 256
ROUTER_TM = 512
DISPATCH_TM = 256
FFN_TM = 256
COMBINE_TM = 128
ADALN_TN = 1024


@dataclasses.dataclass(frozen=True)
class Geom:
    bp: int
    tp: int
    bs: int
    ts: int

    @property
    def n_p(self):
        return self.bp * self.tp

    @property
    def n(self):
        return self.bp * self.tp + self.bs * self.ts

    @property
    def nb(self):
        return self.bp + self.bs

    def locate(self, row):
        in_p = row < self.n_p
        rs = jnp.maximum(row - self.n_p, 0)
        b = jnp.where(in_p, row // self.tp, self.bp + rs // self.ts)
        pos = jnp.where(in_p, row % self.tp, rs % self.ts)
        return b, pos

    def seq_len(self, row):
        return jnp.where(row < self.n_p, self.tp, self.ts)


def _cparams(sem, vmem=VMEM_LIMIT_BYTES):
    return pltpu.CompilerParams(dimension_semantics=sem, vmem_limit_bytes=vmem)


def _dot(a, b):
    return jnp.dot(a, b, preferred_element_type=F32)


def _dot_nt(a, b):
    return lax.dot_general(a, b, (((1,), (1,)), ((), ())), preferred_element_type=F32)


def _sigmoid(x):
    return 1.0 / (1.0 + jnp.exp(-x))


def _rms(x, g):
    return x * lax.rsqrt(jnp.mean(x * x, axis=-1, keepdims=True) + EPS) * g


def _adaln_kernel(c_ref, w_ref, b_ref, o_ref):
    c = c_ref[...]
    s = c * _sigmoid(c)
    o_ref[...] = jnp.dot(s, w_ref[...], preferred_element_type=F32,
                         precision=lax.Precision.HIGHEST) + b_ref[...]


def _adaln(c_pad, w_ada, b_ada):
    depth, d, w = w_ada.shape
    rows = c_pad.shape[0]
    return pl.pallas_call(
        _adaln_kernel,
        out_shape=jax.ShapeDtypeStruct((depth, rows, w), F32),
        grid=(depth, w // ADALN_TN),
        in_specs=[
            pl.BlockSpec((rows, d), lambda l, j: (0, 0)),
            pl.BlockSpec((None, d, ADALN_TN), lambda l, j: (l, 0, j)),
            pl.BlockSpec((None, 1, ADALN_TN), lambda l, j: (l, 0, j)),
        ],
        out_specs=pl.BlockSpec((None, rows, ADALN_TN), lambda l, j: (l, 0, j)),
        compiler_params=_cparams(("parallel", "parallel")),
        name="adaln",
    )(c_pad, w_ada, b_ada.reshape(depth, 1, w))


def _rope_ret(x, cos, sin):
    return x * cos + pltpu.roll(x, HEAD_DIM // 2, 1) * sin


def _rope_axial(x, cos, sin):
    lane = lax.broadcasted_iota(I32, x.shape, 1)
    first = (lane & (HEAD_DIM // 4)) == 0
    rot = jnp.where(first, pltpu.roll(x, HEAD_DIM - HEAD_DIM // 4, 1), pltpu.roll(x, HEAD_DIM // 4, 1))
    return x * cos + rot * sin


def _inproj_kernel(x_ref, sc_ref, sh_ref, g1_ref, w_ref, rc_ref, rs_ref, ac_ref, as_ref, gains_ref,
                   o_ref, h_scr, acc_scr):
    j = pl.program_id(1)

    @pl.when(j == 0)
    def _():
        h = _rms(x_ref[...], g1_ref[...]) * (1.0 + sc_ref[...]) + sh_ref[...]
        h_scr[...] = h.astype(BF16)

    acc_scr[...] = _dot(h_scr[...], w_ref[...])
    heads = PROJ_TN // HEAD_DIM

    def head(hh):
        return acc_scr[:, hh * HEAD_DIM:(hh + 1) * HEAD_DIM]

    def put(hh, val):
        o_ref[:, hh * HEAD_DIM:(hh + 1) * HEAD_DIM] = val.astype(BF16)

    @pl.when(j == 0)
    def _():
        for hh in range(heads):
            put(hh, _rope_ret(head(hh), rc_ref[...], rs_ref[...]))

    @pl.when(j == 1)
    def _():
        for hh in range(heads):
            put(hh, _rope_ret(head(hh), rc_ref[...], rs_ref[...]) * QK_SCALE)

    @pl.when((j == 2) | (j == 6))
    def _():
        o_ref[...] = acc_scr[...].astype(BF16)

    @pl.when(j == 3)
    def _():
        a = acc_scr[...]
        o_ref[...] = (a * _sigmoid(a)).astype(BF16)

    @pl.when(j == 4)
    def _():
        for hh in range(heads):
            put(hh, _rms(head(hh), gains_ref[0:1, :]))

    @pl.when(j == 5)
    def _():
        for hh in range(heads):
            put(hh, _rms(head(hh), gains_ref[1:2, :]))

    @pl.when((j == 7) | (j == 8))
    def _():
        for hh in range(heads):
            put(hh, _rope_axial(_rms(head(hh), gains_ref[2:3, :]), ac_ref[...], as_ref[...]) * QK_SCALE)

    @pl.when(j == 9)
    def _():
        for hh in range(2):
            put(hh, _rope_axial(_rms(head(hh), gains_ref[3:4, :]), ac_ref[...], as_ref[...]))
        o_ref[:, 2 * HEAD_DIM:] = acc_scr[:, 2 * HEAD_DIM:].astype(BF16)


def _inproj(x, mod5, layer, g1, w_in_bf, tabs, gains, geo):
    n = x.shape[0]
    tm = PROJ_TM

    def mod_map(chunk):
        def f(i, j):
            b, _ = geo.locate(i * tm)
            return (layer, chunk, b, 0, 0)
        return f

    def tab_map(i, j):
        _, pos = geo.locate(i * tm)
        return (pos // tm, 0)

    mod_spec = lambda chunk: pl.BlockSpec((None, None, None, 1, D_MODEL), mod_map(chunk))
    tab_spec = pl.BlockSpec((tm, HEAD_DIM), tab_map)
    return pl.pallas_call(
        _inproj_kernel,
        out_shape=jax.ShapeDtypeStruct((n, PROJ_W), BF16),
        grid=(n // tm, PROJ_W // PROJ_TN),
        in_specs=[
            pl.BlockSpec((tm, D_MODEL), lambda i, j: (i, 0)),
            mod_spec(1), mod_spec(0),
            pl.BlockSpec((1, D_MODEL), lambda i, j: (0, 0)),
            pl.BlockSpec((D_MODEL, PROJ_TN), lambda i, j: (0, j)),
            tab_spec, tab_spec, tab_spec, tab_spec,
            pl.BlockSpec((4, HEAD_DIM), lambda i, j: (0, 0)),
        ],
        out_specs=pl.BlockSpec((tm, PROJ_TN), lambda i, j: (i, j)),
        scratch_shapes=[pltpu.VMEM((tm, D_MODEL), BF16), pltpu.VMEM((tm, PROJ_TN), F32)],
        compiler_params=_cparams(("parallel", "arbitrary")),
        name="inproj",
    )(x, mod5, mod5, g1, w_in_bf, tabs[0], tabs[1], tabs[2], tabs[3], gains)


def _ret_kernel(lg_ref, q_ref, k_ref, v_ref, *rest, reverse, geo):
    if reverse:
        o_ref, s_scr, tab_scr = rest
    else:
        sg_ref, ob_ref, gn_ref, o_ref, s_scr, tab_scr = rest
    i = pl.program_id(0)
    nchunks = pl.num_programs(0)
    c = RET_CHUNK
    heads = RET_W // HEAD_DIM

    @pl.when(i == 0)
    def _():
        row = lax.broadcasted_iota(I32, (c, c), 0).astype(F32)
        col = lax.broadcasted_iota(I32, (c, c), 1).astype(F32)
        for hh in range(heads):
            lg = lg_ref[hh]
            if reverse:
                diff = col - row
                decay = jnp.where(diff > 0, jnp.exp(lg * jnp.maximum(diff, 0.0)), 0.0)
                qdec = jnp.exp(lg * (c - row))
                kdec = jnp.exp(lg * row)
            else:
                diff = row - col
                decay = jnp.where(diff >= 0, jnp.exp(lg * jnp.maximum(diff, 0.0)), 0.0)
                qdec = jnp.exp(lg * (row + 1.0))
                kdec = jnp.exp(lg * (c - 1.0 - row))
            tab_scr[hh, 0] = decay
            tab_scr[hh, 1] = qdec
            tab_scr[hh, 2] = kdec
            tab_scr[hh, 3] = jnp.exp(jnp.zeros((c, c), F32) + lg * c)

    chunk = (nchunks - 1 - i) if reverse else i
    row0 = chunk * c
    _, pos = geo.locate(row0)
    boundary = (pos + c == geo.seq_len(row0)) if reverse else (pos == 0)

    @pl.when(boundary)
    def _():
        s_scr[...] = jnp.zeros_like(s_scr)

    for hh in range(heads):
        sl = slice(hh * HEAD_DIM, (hh + 1) * HEAD_DIM)
        qh = q_ref[:, sl]
        kh = k_ref[:, sl]
        vh = v_ref[:, sl]
        s = _dot_nt(qh, kh) * tab_scr[hh, 0]
        o = _dot(s.astype(BF16), vh)
        qd = (qh.astype(F32) * tab_scr[hh, 1]).astype(BF16)
        o = o + _dot(qd, s_scr[hh].astype(BF16))
        kd_t = (kh.astype(F32) * tab_scr[hh, 2]).T.astype(BF16)
        s_scr[hh] = s_scr[hh] * tab_scr[hh, 3] + _dot(kd_t, vh)
        if reverse:
            o_ref[:, sl] = o
        else:
            tot = o + ob_ref[:, sl]
            y = _rms(tot, gn_ref[:, sl])
            o_ref[:, sl] = (sg_ref[:, sl].astype(F32) * y).astype(BF16)


def _retention(proj, lg_f, lg_b, ret_gn, geo):
    n = proj.shape[0]
    nchunks = n // RET_CHUNK
    c = RET_CHUNK
    heads = RET_W // HEAD_DIM
    scratch = [pltpu.VMEM((heads, HEAD_DIM, HEAD_DIM), F32), pltpu.VMEM((heads, 4, c, c), F32)]
    smem = pl.BlockSpec(memory_space=pltpu.SMEM)

    def col_spec(colblk, rev):
        if rev:
            return pl.BlockSpec((c, RET_W), lambda i: (nchunks - 1 - i, colblk))
        return pl.BlockSpec((c, RET_W), lambda i: (i, colblk))

    o_bwd = pl.pallas_call(
        functools.partial(_ret_kernel, reverse=True, geo=geo),
        out_shape=jax.ShapeDtypeStruct((n, RET_W), F32),
        grid=(nchunks,),
        in_specs=[smem, col_spec(0, True), col_spec(1, True), col_spec(2, True)],
        out_specs=pl.BlockSpec((c, RET_W), lambda i: (nchunks - 1 - i, 0)),
        scratch_shapes=scratch,
        compiler_params=_cparams(("arbitrary",)),
        name="ret_bwd",
    )(lg_b, proj, proj, proj)
    return pl.pallas_call(
        functools.partial(_ret_kernel, reverse=False, geo=geo),
        out_shape=jax.ShapeDtypeStruct((n, RET_W), BF16),
        grid=(nchunks,),
        in_specs=[smem, col_spec(0, False), col_spec(1, False), col_spec(2, False), col_spec(3, False),
                  pl.BlockSpec((c, RET_W), lambda i: (i, 0)),
                  pl.BlockSpec((1, RET_W), lambda i: (0, 0))],
        out_specs=pl.BlockSpec((c, RET_W), lambda i: (i, 0)),
        scratch_shapes=scratch,
        compiler_params=_cparams(("arbitrary",)),
        name="ret_fwd",
    )(lg_f, proj, proj, proj, proj, o_bwd, ret_gn)


def _na_geometry(i, geo):
    rp, rs = geo.tp // GRID_W, geo.ts // GRID_W
    r0g = i * NA_QROWS
    in_p = r0g < geo.bp * rp
    rsmp = jnp.maximum(r0g - geo.bp * rp, 0)
    rows = jnp.where(in_p, rp, rs)
    lr0 = jnp.where(in_p, r0g % rp, rsmp % rs)
    slab = jnp.clip(lr0 - NA_WIN_R // 2, 0, rows - NA_SLAB_ROWS)
    return rows, lr0, slab, r0g - lr0


def _na_kernel(q_ref, k0, k1, k2, k3, v0, v1, v2, v3, tab_ref, on_ref, o_ref, ks_scr, vs_scr, *, geo):
    i = pl.program_id(0)
    rows, lr0, slab, _ = _na_geometry(i, geo)
    blk = NA_KBLK_ROWS * GRID_W
    for m, (kr, vr) in enumerate(((k0, v0), (k1, v1), (k2, v2), (k3, v3))):
        ks_scr[m * blk:(m + 1) * blk, :] = kr[...]
        vs_scr[m * blk:(m + 1) * blk, :] = vr[...]
    heads = NA_W // HEAD_DIM
    win = NA_WIN_R * GRID_W

    def body(p, carry):
        r = lr0 + p
        rs = jnp.clip(r - NA_WIN_R // 2, 0, rows - NA_WIN_R)
        off = pl.multiple_of((rs - slab) * GRID_W, GRID_W)
        var = r - rs
        qrow = q_ref[pl.ds(pl.multiple_of(p * GRID_W, GRID_W), GRID_W), :]
        outs = []
        for hh in range(heads):
            sl = slice(hh * HEAD_DIM, (hh + 1) * HEAD_DIM)
            kw = ks_scr[pl.ds(off, win), sl]
            vw = vs_scr[pl.ds(off, win), sl]
            s = _dot_nt(qrow[:, sl], kw) * QK_SCALE + tab_ref[hh, var]
            e = jnp.exp(s - jnp.max(s, axis=-1, keepdims=True))
            prob = e / jnp.sum(e, axis=-1, keepdims=True)
            outs.append(_dot(prob.astype(BF16), vw))
        o = jnp.concatenate(outs, axis=-1)
        o_ref[pl.ds(pl.multiple_of(p * GRID_W, GRID_W), GRID_W), :] = _rms(o, on_ref[...]).astype(BF16)
        return carry

    lax.fori_loop(0, NA_QROWS, body, 0)


def _na_bias_table(rpb):
    rpb = rpb.astype(F32)
    rows = jnp.stack([rpb[:, NA_WIN_R - 1 - v:2 * NA_WIN_R - 1 - v, :] for v in range(NA_WIN_R)], axis=1)
    width = 2 * GRID_W
    lead = GRID_W - NA_WIN_C
    p = jnp.pad(rows, ((0, 0), (0, 0), (0, 0), (lead, width - lead - (2 * NA_WIN_C - 1))))
    skew = jnp.tile(p, GRID_W)[..., :GRID_W * (width - 1)].reshape(p.shape[:-1] + (GRID_W, width - 1))
    toe = skew[..., GRID_W - 1:]
    c = np.arange(GRID_W)[:, None]
    j = np.arange(GRID_W)[None, :]
    cs = np.clip(c - NA_WIN_C // 2, 0, GRID_W - NA_WIN_C)
    valid = (j >= cs) & (j < cs + NA_WIN_C)
    tab = jnp.where(valid[None, None, :, None, :], toe.transpose(0, 1, 3, 2, 4), MASK_VALUE)
    return tab.reshape(rpb.shape[0], NA_WIN_R, GRID_W, NA_WIN_R * GRID_W)


def _neighbourhood(proj, bias_tab, na_on, geo):
    n = proj.shape[0]
    tq = NA_QROWS * GRID_W
    blk = NA_KBLK_ROWS * GRID_W
    nblk = NA_SLAB_ROWS // NA_KBLK_ROWS
    qcol, kcol, vcol = 4, 5, 6

    def kv_spec(colblk, m):
        def f(i):
            _, _, slab, seq_row0 = _na_geometry(i, geo)
            return ((seq_row0 + slab) // NA_KBLK_ROWS + m, colblk)
        return pl.BlockSpec((blk, NA_W), f)

    heads = NA_W // HEAD_DIM
    return pl.pallas_call(
        functools.partial(_na_kernel, geo=geo),
        out_shape=jax.ShapeDtypeStruct((n, NA_W), BF16),
        grid=(n // tq,),
        in_specs=[pl.BlockSpec((tq, NA_W), lambda i: (i, qcol))]
        + [kv_spec(kcol, m) for m in range(nblk)]
        + [kv_spec(vcol, m) for m in range(nblk)]
        + [pl.BlockSpec((heads, NA_WIN_R, GRID_W, NA_WIN_R * GRID_W), lambda i: (0, 0, 0, 0)),
           pl.BlockSpec((1, NA_W), lambda i: (0, 0))],
        out_specs=pl.BlockSpec((tq, NA_W), lambda i: (i, 0)),
        scratch_shapes=[pltpu.VMEM((NA_SLAB_ROWS * GRID_W, NA_W), BF16),
                        pltpu.VMEM((NA_SLAB_ROWS * GRID_W, NA_W), BF16)],
        compiler_params=_cparams(("parallel",)),
        name="natten",
    )(*([proj] * (1 + 2 * nblk)), bias_tab, na_on)


def _flash_kernel(q_ref, k_ref, v_ref, o_ref, m_scr, l_scr, acc_scr):
    ki = pl.program_id(3)
    groups = q_ref.shape[1] // HEAD_DIM

    @pl.when(ki == 0)
    def _():
        m_scr[...] = jnp.full_like(m_scr, NEG_INF)
        l_scr[...] = jnp.zeros_like(l_scr)
        acc_scr[...] = jnp.zeros_like(acc_scr)

    k = k_ref[...]
    v = v_ref[...]
    for gq in range(groups):
        q = q_ref[:, gq * HEAD_DIM:(gq + 1) * HEAD_DIM]
        s = _dot_nt(q, k)
        m_prev = m_scr[gq][:, :1]
        m_next = jnp.maximum(m_prev, jnp.max(s, axis=1, keepdims=True))
        alpha = jnp.exp(m_prev - m_next)
        p = jnp.exp(s - m_next)
        l_scr[gq] = alpha * l_scr[gq] + jnp.sum(p, axis=1, keepdims=True)
        acc_scr[gq] = alpha * acc_scr[gq] + _dot(p.astype(BF16), v)
        m_scr[gq] = jnp.broadcast_to(m_next, m_scr.shape[1:])

    @pl.when(ki == pl.num_programs(3) - 1)
    def _():
        for gq in range(groups):
            o_ref[:, gq * HEAD_DIM:(gq + 1) * HEAD_DIM] = (acc_scr[gq] / l_scr[gq]).astype(BF16)


def _flash_group(proj, row0, batch, t):
    tq, tk = min(FLASH_TQ, t), min(FLASH_TK, t)
    kv_heads = GA_KV_W // HEAD_DIM
    qw = GA_W // kv_heads
    groups = qw // HEAD_DIM
    qcol0 = (4 * RET_W + 3 * NA_W) // qw
    kcol0 = (4 * RET_W + 3 * NA_W + GA_W) // HEAD_DIM
    vcol0 = kcol0 + kv_heads
    return pl.pallas_call(
        _flash_kernel,
        out_shape=jax.ShapeDtypeStruct((batch * t, GA_W), BF16),
        grid=(batch, kv_heads, t // tq, t // tk),
        in_specs=[
            pl.BlockSpec((tq, qw), lambda b, h, qi, ki: ((row0 + b * t) // tq + qi, qcol0 + h)),
            pl.BlockSpec((tk, HEAD_DIM), lambda b, h, qi, ki: ((row0 + b * t) // tk + ki, kcol0 + h)),
            pl.BlockSpec((tk, HEAD_DIM), lambda b, h, qi, ki: ((row0 + b * t) // tk + ki, vcol0 + h)),
        ],
        out_specs=pl.BlockSpec((tq, qw), lambda b, h, qi, ki: ((b * t) // tq + qi, h)),
        scratch_shapes=[pltpu.VMEM((groups, tq, HEAD_DIM), F32),
                        pltpu.VMEM((groups, tq, HEAD_DIM), F32),
                        pltpu.VMEM((groups, tq, HEAD_DIM), F32)],
        compiler_params=_cparams(("parallel", "parallel", "parallel", "arbitrary")),
        name="flash_gqa",
    )(proj, proj, proj)


def _outproj_kernel(yr_ref, yn_ref, yg_ref, x_ref, w_ref, gon_ref, gt_ref, sc_ref, sh_ref, g2_ref,
                    x1_ref, h2_ref):
    ygn = _rms(yg_ref[...].astype(F32), gon_ref[...]).astype(BF16)
    acc = _dot(yr_ref[...], w_ref[0:RET_W, :])
    acc = acc + _dot(yn_ref[...], w_ref[RET_W:RET_W + NA_W, :])
    acc = acc + _dot(ygn, w_ref[RET_W + NA_W:, :])
    x1 = x_ref[...] + gt_ref[...] * acc
    x1_ref[...] = x1
    h2_ref[...] = _rms(x1, g2_ref[...]) * (1.0 + sc_ref[...]) + sh_ref[...]


def _outproj(y_ret, y_na, y_ga, x, w_out_bf, ga_on, mod5, layer, g2, geo):
    n = x.shape[0]
    tm = OUT_TM

    def mod_spec(chunk):
        def f(i):
            b, _ = geo.locate(i * tm)
            return (layer, chunk, b, 0, 0)
        return pl.BlockSpec((None, None, None, 1, D_MODEL), f)

    row = lambda w: pl.BlockSpec((tm, w), lambda i: (i, 0))
    return pl.pallas_call(
        _outproj_kernel,
        out_shape=(jax.ShapeDtypeStruct((n, D_MODEL), F32), jax.ShapeDtypeStruct((n, D_MODEL), F32)),
        grid=(n // tm,),
        in_specs=[row(RET_W), row(NA_W), row(GA_W), row(D_MODEL),
                  pl.BlockSpec((D_MODEL, D_MODEL), lambda i: (0, 0)),
                  pl.BlockSpec((1, GA_W), lambda i: (0, 0)),
                  mod_spec(2), mod_spec(4), mod_spec(3),
                  pl.BlockSpec((1, D_MODEL), lambda i: (0, 0))],
        out_specs=(row(D_MODEL), row(D_MODEL)),
        compiler_params=_cparams(("parallel",)),
        name="outproj",
    )(y_ret, y_na, y_ga, x, w_out_bf, ga_on, mod5, mod5, mod5, g2)


def _first_index_of_max(vals, ids, axes, sentinel):
    m = vals
    for ax in axes:
        m = jnp.max(m, axis=ax, keepdims=True)
    cand = jnp.where(vals == m, ids, sentinel)
    for ax in axes:
        cand = jnp.min(cand, axis=ax, keepdims=True)
    return m, cand


def _router_kernel(h_ref, w_ref, b_ref, idx_ref, wgt_ref, rank_ref, cnt_ref, cnt_scr, tri_scr):
    i = pl.program_id(0)
    tm = h_ref.shape[0]

    @pl.when(i == 0)
    def _():
        cnt_scr[...] = jnp.zeros_like(cnt_scr)
        r = lax.broadcasted_iota(I32, (tm, tm), 0)
        c = lax.broadcasted_iota(I32, (tm, tm), 1)
        tri_scr[...] = jnp.where(r < c, 1.0, 0.0).astype(BF16)

    logits = lax.dot_general(w_ref[...], h_ref[...], (((1,), (1,)), ((), ())),
                             preferred_element_type=F32, precision=lax.Precision.HIGHEST)
    scores = _sigmoid(logits)
    choice = scores + b_ref[...][:, :1]
    shape3 = (N_GROUPS, GROUP_SIZE, tm)
    choice3 = choice.reshape(shape3)
    scores3 = scores.reshape(shape3)
    sub = lax.broadcasted_iota(I32, shape3, 1)
    grp = lax.broadcasted_iota(I32, shape3, 0)
    eid = grp * GROUP_SIZE + sub

    m1, i1 = _first_index_of_max(choice3, sub, (1,), GROUP_SIZE)
    rest = jnp.where(sub == i1, NEG_INF, choice3)
    m2 = jnp.max(rest, axis=1, keepdims=True)
    gscore = m1 + m2

    gid = lax.broadcasted_iota(I32, gscore.shape, 0)
    gsel = jnp.zeros(gscore.shape, F32)
    for _ in range(TOPK_GROUPS):
        _, gi = _first_index_of_max(gscore, gid, (0,), N_GROUPS)
        hit = gid == gi
        gsel = jnp.where(hit, 1.0, gsel)
        gscore = jnp.where(hit, NEG_INF, gscore)

    masked = jnp.where(gsel > 0.0, choice3, NEG_INF)
    onehots, ids, ws = [], [], []
    for _ in range(TOP_K):
        _, ei = _first_index_of_max(masked, eid, (1, 0), N_EXPERTS)
        hit = eid == ei
        onehots.append(hit)
        ids.append(ei.reshape(1, tm))
        ws.append(jnp.sum(jnp.where(hit, scores3, 0.0), axis=(0, 1), keepdims=True).reshape(1, tm))
        masked = jnp.where(hit, NEG_INF, masked)
    wsum = ws[0]
    for k in range(1, TOP_K):
        wsum = wsum + ws[k]

    sel = jnp.zeros(shape3, F32)
    for hit in onehots:
        sel = jnp.where(hit, 1.0, sel)
    sel2 = sel.reshape(N_EXPERTS, tm)
    before = _dot(sel2.astype(BF16), tri_scr[...]) + cnt_scr[:, :1]
    before3 = before.reshape(shape3)
    ranks = [jnp.sum(jnp.where(hit, before3, 0.0), axis=(0, 1), keepdims=True).reshape(1, tm)
             for hit in onehots]
    cnt_scr[...] = cnt_scr[...] + jnp.sum(sel2, axis=1, keepdims=True)

    idx_ref[...] = jnp.concatenate(ids, axis=0)
    wgt_ref[...] = jnp.concatenate([w / wsum * ROUTED_SCALE for w in ws], axis=0)
    rank_ref[...] = jnp.concatenate(ranks, axis=0).astype(I32)
    cnt_ref[...] = cnt_scr[...]


def _router(h2, w_router_t, bias_col):
    n = h2.shape[0]
    tm = ROUTER_TM
    out_blk = pl.BlockSpec((TOP_K, tm), lambda i: (0, i))
    return pl.pallas_call(
        _router_kernel,
        out_shape=(jax.ShapeDtypeStruct((TOP_K, n), I32), jax.ShapeDtypeStruct((TOP_K, n), F32),
                   jax.ShapeDtypeStruct((TOP_K, n), I32), jax.ShapeDtypeStruct((N_EXPERTS, HEAD_DIM), F32)),
        grid=(n // tm,),
        in_specs=[pl.BlockSpec((tm, D_MODEL), lambda i: (i, 0)),
                  pl.BlockSpec((N_EXPERTS, D_MODEL), lambda i: (0, 0)),
                  pl.BlockSpec((N_EXPERTS, HEAD_DIM), lambda i: (0, 0))],
        out_specs=(out_blk, out_blk, out_blk, pl.BlockSpec((N_EXPERTS, HEAD_DIM), lambda i: (0, 0))),
        scratch_shapes=[pltpu.VMEM((N_EXPERTS, HEAD_DIM), F32), pltpu.VMEM((tm, tm), BF16)],
        compiler_params=_cparams(("arbitrary",)),
        name="router",
    )(h2, w_router_t, bias_col)


def _row_copy(src, src_row, dst, dst_row, sem, rows=1):
    return pltpu.make_async_copy(src.at[pl.ds(src_row, rows)], dst.at[pl.ds(dst_row, rows)], sem)


def _zero_fill_padding(fill_ref, end_ref, xs_hbm, zero_scr, zsem, wait):
    def go(copy):
        copy.wait() if wait else copy.start()

    def per_expert(e, carry):
        def per_row(r, c):
            go(pltpu.make_async_copy(zero_scr.at[pl.ds(0, 1)], xs_hbm.at[pl.ds(r, 1)], zsem))
            return c
        return lax.fori_loop(fill_ref[e], end_ref[e], per_row, carry)

    lax.fori_loop(0, N_EXPERTS, per_expert, 0)
    total = end_ref[N_EXPERTS - 1]
    n_tiles = xs_hbm.shape[0] // FFN_TM
    for tile in range(n_tiles - N_EXPERTS, n_tiles):
        @pl.when(tile * FFN_TM >= total)
        def _(tile=tile):
            go(pltpu.make_async_copy(zero_scr, xs_hbm.at[pl.ds(tile * FFN_TM, FFN_TM)], zsem))


def _dispatch_kernel(seg_ref, fill_ref, end_ref, idx_ref, rank_ref, h_ref, xs_hbm, zero_scr, sem, zsem):
    i = pl.program_id(0)
    tm = idx_ref.shape[1]

    @pl.when(i == 0)
    def _():
        zero_scr[...] = jnp.zeros_like(zero_scr)
        _zero_fill_padding(fill_ref, end_ref, xs_hbm, zero_scr, zsem, wait=False)
        _zero_fill_padding(fill_ref, end_ref, xs_hbm, zero_scr, zsem, wait=True)

    def issue(t, carry):
        for k in range(TOP_K):
            dst = seg_ref[idx_ref[k, t]] + rank_ref[k, t]
            _row_copy(h_ref, t, xs_hbm, dst, sem).start()
        return carry

    lax.fori_loop(0, tm, issue, 0)
    def drain(t, carry):
        _row_copy(h_ref, 0, xs_hbm, 0, sem, rows=TOP_K).wait()
        return carry

    lax.fori_loop(0, tm, drain, 0)


def _dispatch(h2, idx, rank, seg_start, seg_fill, seg_end, n_slots):
    n = h2.shape[0]
    tm = DISPATCH_TM
    smem_blk = pl.BlockSpec((TOP_K, tm), lambda i, *_: (0, i), memory_space=pltpu.SMEM)
    return pl.pallas_call(
        _dispatch_kernel,
        out_shape=jax.ShapeDtypeStruct((n_slots, D_MODEL), F32),
        grid_spec=pltpu.PrefetchScalarGridSpec(
            num_scalar_prefetch=3,
            grid=(n // tm,),
            in_specs=[smem_blk, smem_blk, pl.BlockSpec((tm, D_MODEL), lambda i, *_: (i, 0))],
            out_specs=pl.BlockSpec(memory_space=pl.ANY),
            scratch_shapes=[pltpu.VMEM((FFN_TM, D_MODEL), F32),
                            pltpu.SemaphoreType.DMA(()), pltpu.SemaphoreType.DMA(())],
        ),
        compiler_params=_cparams(("arbitrary",)),
        name="moe_dispatch",
    )(seg_start, seg_fill, seg_end, idx, rank, h2)


def _ffn_kernel(te_ref, nv_ref, x_ref, wg_ref, wu_ref, wd_ref, o_ref):
    i = pl.program_id(0)
    nvalid = nv_ref[i]

    @pl.when(nvalid > 0)
    def _():
        x = x_ref[...].astype(BF16)
        hg = _dot(x, wg_ref[...])
        hu = _dot(x, wu_ref[...])
        h = (hg * _sigmoid(hg) * hu).astype(BF16)
        o_ref[...] = _dot(h, wd_ref[...])

    @pl.when(nvalid == 0)
    def _():
        o_ref[...] = jnp.zeros_like(o_ref)


def _expert_ffn(x_sorted, tile_expert, tile_valid, wg, wu, wd):
    n_slots = x_sorted.shape[0]
    tm = FFN_TM
    return pl.pallas_call(
        _ffn_kernel,
        out_shape=jax.ShapeDtypeStruct((n_slots, D_MODEL), F32),
        grid_spec=pltpu.PrefetchScalarGridSpec(
            num_scalar_prefetch=2,
            grid=(n_slots // tm,),
            in_specs=[pl.BlockSpec((tm, D_MODEL), lambda i, te, nv: (i, 0)),
                      pl.BlockSpec((None, D_MODEL, EXPERT_FF), lambda i, te, nv: (te[i], 0, 0)),
                      pl.BlockSpec((None, D_MODEL, EXPERT_FF), lambda i, te, nv: (te[i], 0, 0)),
                      pl.BlockSpec((None, EXPERT_FF, D_MODEL), lambda i, te, nv: (te[i], 0, 0))],
            out_specs=pl.BlockSpec((tm, D_MODEL), lambda i, te, nv: (i, 0)),
        ),
        compiler_params=_cparams(("parallel",)),
        name="moe_ffn",
    )(tile_expert, tile_valid, x_sorted, wg, wu, wd)


def _combine_kernel(seg_ref, idx_ref, rank_ref, idxn_ref, rankn_ref, wgt_ref, x1_ref, h_ref, gt_ref,
                    wsg_ref, wsu_ref, wsd_ref, y_hbm, o_ref, ybuf, sems):
    i = pl.program_id(0)
    nsteps = pl.num_programs(0)
    tm = x1_ref.shape[0]

    def issue(slot, ir, rr):
        def body(t, carry):
            for k in range(TOP_K):
                src = seg_ref[ir[k, t]] + rr[k, t]
                pltpu.make_async_copy(y_hbm.at[pl.ds(src, 1)], ybuf.at[slot, k, pl.ds(t, 1)],
                                      sems.at[slot]).start()
            return carry
        lax.fori_loop(0, tm, body, 0)

    slot = i % 2

    @pl.when(i == 0)
    def _():
        issue(0, idx_ref, rank_ref)

    @pl.when(i + 1 < nsteps)
    def _():
        issue(1 - slot, idxn_ref, rankn_ref)

    hb = h_ref[...].astype(BF16)
    hg = _dot(hb, wsg_ref[...])
    hu = _dot(hb, wsu_ref[...])
    shared = _dot((hg * _sigmoid(hg) * hu).astype(BF16), wsd_ref[...])

    for k in range(TOP_K):
        pltpu.make_async_copy(y_hbm.at[pl.ds(0, tm)], ybuf.at[slot, k], sems.at[slot]).wait()

    acc = shared
    for k in range(TOP_K):
        acc = acc + wgt_ref[:, k:k + 1] * ybuf[slot, k]
    o_ref[...] = x1_ref[...] + gt_ref[...] * acc


def _combine(y_sorted, idx, rank, wgt_t, seg_start, x1, h2, mod5, layer, wsg, wsu, wsd, geo):
    n = x1.shape[0]
    tm = COMBINE_TM
    nsteps = n // tm
    smem_cur = pl.BlockSpec((TOP_K, tm), lambda i, seg: (0, i), memory_space=pltpu.SMEM)
    smem_next = pl.BlockSpec((TOP_K, tm), lambda i, seg: (0, jnp.minimum(i + 1, nsteps - 1)),
                             memory_space=pltpu.SMEM)

    def gt_map(i, seg):
        b, _ = geo.locate(i * tm)
        return (layer, 5, b, 0, 0)

    row = pl.BlockSpec((tm, D_MODEL), lambda i, seg: (i, 0))
    whole = lambda a, b: pl.BlockSpec((a, b), lambda i, seg: (0, 0))
    return pl.pallas_call(
        _combine_kernel,
        out_shape=jax.ShapeDtypeStruct((n, D_MODEL), F32),
        grid_spec=pltpu.PrefetchScalarGridSpec(
            num_scalar_prefetch=1,
            grid=(nsteps,),
            in_specs=[smem_cur, smem_cur, smem_next, smem_next,
                      pl.BlockSpec((tm, TOP_K), lambda i, seg: (i, 0)),
                      row, row,
                      pl.BlockSpec((None, None, None, 1, D_MODEL), gt_map),
                      whole(D_MODEL, EXPERT_FF), whole(D_MODEL, EXPERT_FF), whole(EXPERT_FF, D_MODEL),
                      pl.BlockSpec(memory_space=pl.ANY)],
            out_specs=row,
            scratch_shapes=[pltpu.VMEM((2, TOP_K, tm, D_MODEL), F32), pltpu.SemaphoreType.DMA((2,))],
        ),
        compiler_params=_cparams(("arbitrary",)),
        name="moe_combine",
    )(seg_start, idx, rank, idx, rank, wgt_t, x1, h2, mod5, wsg, wsu, wsd, y_sorted)


def _rope_tables(t_max):
    pos = jnp.arange(t_max, dtype=jnp.int32)

    def angles(p, n_pairs):
        inv = ROPE_BASE ** (-jnp.arange(n_pairs, dtype=F32) / n_pairs)
        ang = p.astype(F32)[:, None] * inv[None, :]
        return jnp.cos(ang), jnp.sin(ang)

    c, s = angles(pos, HEAD_DIM // 2)
    cr, sr = angles(pos // GRID_W, HEAD_DIM // 4)
    cc, sc = angles(pos % GRID_W, HEAD_DIM // 4)
    return (jnp.concatenate([c, c], -1), jnp.concatenate([-s, s], -1),
            jnp.concatenate([cr, cr, cc, cc], -1), jnp.concatenate([-sr, sr, -sc, sc], -1))


def _moe_plan(counts, n_slots):
    padded = (counts + FFN_TM - 1) // FFN_TM * FFN_TM
    seg_end = jnp.cumsum(padded)
    seg_start = seg_end - padded
    tile_row0 = jnp.arange(n_slots // FFN_TM, dtype=I32) * FFN_TM
    tile_expert = jnp.minimum(jnp.sum(tile_row0[:, None] >= seg_end[None, :], axis=1), N_EXPERTS - 1).astype(I32)
    tile_valid = jnp.clip(seg_start[tile_expert] + counts[tile_expert] - tile_row0, 0, FFN_TM).astype(I32)
    return (seg_start.astype(I32), (seg_start + counts).astype(I32), seg_end.astype(I32),
            tile_expert, tile_valid)


def _layer(x, mod5, layer, p, tabs, geo):
    n = x.shape[0]
    proj = _inproj(x, mod5, layer, p["g1"], p["w_in"], tabs, p["gains"], geo)
    y_ret = _retention(proj, p["lg_f"], p["lg_b"], p["ret_gn"], geo)
    y_na = _neighbourhood(proj, p["na_tab"], p["na_on"], geo)
    y_ga = jnp.concatenate([_flash_group(proj, 0, geo.bp, geo.tp),
                            _flash_group(proj, geo.n_p, geo.bs, geo.ts)], axis=0)
    x1, h2 = _outproj(y_ret, y_na, y_ga, x, p["w_out"], p["ga_on"], mod5, layer, p["g2"], geo)
    idx, wgt, rank, cnt = _router(h2, p["w_router_t"], p["router_bias"])
    n_slots = n * TOP_K + N_EXPERTS * FFN_TM
    seg_start, seg_fill, seg_end, tile_expert, tile_valid = _moe_plan(cnt[:, 0].astype(I32), n_slots)
    x_sorted = _dispatch(h2, idx, rank, seg_start, seg_fill, seg_end, n_slots)
    y_sorted = _expert_ffn(x_sorted, tile_expert, tile_valid, p["w_eg"], p["w_eu"], p["w_ed"])
    return _combine(y_sorted, idx, rank, wgt.T, seg_start, x1, h2, mod5, layer,
                    p["w_sg"], p["w_su"], p["w_sd"], geo)


def kernel(x_prompt, x_sample, c_prompt, c_sample, w_ada, b_ada, norm1, w_in, ret_decay_fwd, ret_decay_bwd, ret_norm, na_q_norm, na_k_norm, na_rpb, na_out_norm, ga_q_norm, ga_k_norm, ga_out_norm, w_out, norm2, w_router, router_bias, w_exp_gate, w_exp_up, w_exp_down, w_sh_gate, w_sh_up, w_sh_down):
    bp, tp, d = x_prompt.shape
    bs, ts, _ = x_sample.shape
    depth = w_ada.shape[0]
    geo = Geom(bp, tp, bs, ts)
    assert d == D_MODEL and w_in.shape[-1] == PROJ_W
    for t in (tp, ts):
        assert t % max(PROJ_TM, FLASH_TQ, FLASH_TK, NA_SLAB_ROWS * GRID_W) == 0

    x = jnp.concatenate([x_prompt.reshape(bp * tp, d), x_sample.reshape(bs * ts, d)], axis=0)
    c_all = jnp.concatenate([c_prompt, c_sample], axis=0)
    c_pad = jnp.zeros((8 * pl.cdiv(geo.nb, 8), d), F32).at[:geo.nb].set(c_all)
    mod = _adaln(c_pad, w_ada, b_ada)
    rows = mod.shape[1]
    mod5 = mod.reshape(depth, rows, N_MOD, 1, d).transpose(0, 2, 1, 3, 4)
    tabs = _rope_tables(max(tp, ts))

    for l in range(depth):
        p = {
            "g1": norm1[l].reshape(1, d),
            "w_in": w_in[l].astype(BF16),
            "gains": jnp.stack([na_q_norm[l], na_k_norm[l], ga_q_norm[l], ga_k_norm[l]]),
            "lg_f": jax.nn.log_sigmoid(ret_decay_fwd[l].astype(F32)),
            "lg_b": jax.nn.log_sigmoid(ret_decay_bwd[l].astype(F32)),
            "ret_gn": ret_norm[l].reshape(1, RET_W),
            "na_tab": _na_bias_table(na_rpb[l]),
            "na_on": na_out_norm[l].reshape(1, NA_W),
            "ga_on": ga_out_norm[l].reshape(1, GA_W),
            "w_out": w_out[l].astype(BF16),
            "g2": norm2[l].reshape(1, d),
            "w_router_t": w_router[l].astype(F32).T,
            "router_bias": jnp.broadcast_to(router_bias[l].astype(F32)[:, None], (N_EXPERTS, HEAD_DIM)),
            "w_eg": w_exp_gate[l].astype(BF16),
            "w_eu": w_exp_up[l].astype(BF16),
            "w_ed": w_exp_down[l].astype(BF16),
            "w_sg": w_sh_gate[l].astype(BF16),
            "w_su": w_sh_up[l].astype(BF16),
            "w_sd": w_sh_down[l].astype(BF16),
        }
        x = _layer(x, mod5, l, p, tabs, geo)

    y_prompt = x[:geo.n_p].reshape(bp, tp, d)
    y_sample = x[geo.n_p:].reshape(bs, ts, d)
    return (y_prompt, y_sample)
```

```python
import dataclasses
import functools

import numpy as np
import jax
import jax.numpy as jnp
from jax import lax
from jax.experimental import pallas as pl
from jax.experimental.pallas import tpu as pltpu

F32 = jnp.float32
BF16 = jnp.bfloat16
I32 = jnp.int32

D_MODEL = 2048
HEAD_DIM = 128
RET_W = 512
NA_W = 512
GA_W = 1024
GA_KV_W = 256
PROJ_W = 4 * RET_W + 3 * NA_W + GA_W + 2 * GA_KV_W
RET_CHUNK = 128
GRID_W = 64
NA_WIN_R = 8
NA_WIN_C = 16
ROPE_BASE = 10000.0
N_EXPERTS = 64
N_GROUPS = 8
GROUP_SIZE = N_EXPERTS // N_GROUPS
TOPK_GROUPS = 4
TOP_K = 8
EXPERT_FF = 512
ROUTED_SCALE = 2.5
N_MOD = 6
EPS = 1e-6
QK_SCALE = HEAD_DIM ** -0.5
LOG2E = 1.4426950408889634
NEG_INF = float("-inf")
MASK_VALUE = -1e30

VMEM_LIMIT_BYTES = 56 * 1024 * 1024

PROJ_TN = 512
PROJ_TM = 1024
NA_QROWS = 8
NA_SLAB_ROWS = 16
NA_KBLK_ROWS = 4
FLASH_TQ = 1024
FLASH_TK = 4096
FLASH_DOT_ROWS = 256
OUT_TM = 256
ROUTER_TM = 512
DISPATCH_TM = 256
FFN_TM = 512
COMBINE_TM = 128
ADALN_TN = 1024


@dataclasses.dataclass(frozen=True)
class Geom:
    bp: int
    tp: int
    bs: int
    ts: int

    @property
    def n_p(self):
        return self.bp * self.tp

    @property
    def n(self):
        return self.bp * self.tp + self.bs * self.ts

    @property
    def nb(self):
        return self.bp + self.bs

    def locate(self, row):
        in_p = row < self.n_p
        rs = jnp.maximum(row - self.n_p, 0)
        b = jnp.where(in_p, row // self.tp, self.bp + rs // self.ts)
        pos = jnp.where(in_p, row % self.tp, rs % self.ts)
        return b, pos

    def seq_len(self, row):
        return jnp.where(row < self.n_p, self.tp, self.ts)


def _cparams(sem, vmem=VMEM_LIMIT_BYTES):
    return pltpu.CompilerParams(dimension_semantics=sem, vmem_limit_bytes=vmem)


def _dot(a, b):
    return jnp.dot(a, b, preferred_element_type=F32)


def _dot_nt(a, b):
    return lax.dot_general(a, b, (((1,), (1,)), ((), ())), preferred_element_type=F32)


def _sigmoid(x):
    return 1.0 / (1.0 + jnp.exp(-x))


def _rms(x, g):
    return x * lax.rsqrt(jnp.mean(x * x, axis=-1, keepdims=True) + EPS) * g


def _adaln_kernel(c_ref, w_ref, b_ref, o_ref):
    c = c_ref[...]
    s = c * _sigmoid(c)
    o_ref[...] = jnp.dot(s, w_ref[...], preferred_element_type=F32,
                         precision=lax.Precision.HIGHEST) + b_ref[...]


def _adaln(c_pad, w_ada, b_ada):
    depth, d, w = w_ada.shape
    rows = c_pad.shape[0]
    return pl.pallas_call(
        _adaln_kernel,
        out_shape=jax.ShapeDtypeStruct((depth, rows, w), F32),
        grid=(depth, w // ADALN_TN),
        in_specs=[
            pl.BlockSpec((rows, d), lambda l, j: (0, 0)),
            pl.BlockSpec((None, d, ADALN_TN), lambda l, j: (l, 0, j)),
            pl.BlockSpec((None, 1, ADALN_TN), lambda l, j: (l, 0, j)),
        ],
        out_specs=pl.BlockSpec((None, rows, ADALN_TN), lambda l, j: (l, 0, j)),
        compiler_params=_cparams(("parallel", "parallel")),
        name="adaln",
    )(c_pad, w_ada, b_ada.reshape(depth, 1, w))


def _rope_ret(x, cos, sin):
    return x * cos + pltpu.roll(x, HEAD_DIM // 2, 1) * sin


def _rope_axial(x, cos, sin):
    lane = lax.broadcasted_iota(I32, x.shape, 1)
    first = (lane & (HEAD_DIM // 4)) == 0
    rot = jnp.where(first, pltpu.roll(x, HEAD_DIM - HEAD_DIM // 4, 1), pltpu.roll(x, HEAD_DIM // 4, 1))
    return x * cos + rot * sin


def _inproj_kernel(x_ref, sc_ref, sh_ref, g1_ref, w_ref, rc_ref, rs_ref, ac_ref, as_ref, gains_ref,
                   o_ref, h_scr, acc_scr):
    j = pl.program_id(1)

    @pl.when(j == 0)
    def _():
        h = _rms(x_ref[...], g1_ref[...]) * (1.0 + sc_ref[...]) + sh_ref[...]
        h_scr[...] = h.astype(BF16)

    acc_scr[...] = _dot(h_scr[...], w_ref[...])
    heads = PROJ_TN // HEAD_DIM

    def head(hh):
        return acc_scr[:, hh * HEAD_DIM:(hh + 1) * HEAD_DIM]

    def put(hh, val):
        o_ref[:, hh * HEAD_DIM:(hh + 1) * HEAD_DIM] = val.astype(BF16)

    @pl.when(j == 0)
    def _():
        for hh in range(heads):
            put(hh, _rope_ret(head(hh), rc_ref[...], rs_ref[...]))

    @pl.when(j == 1)
    def _():
        for hh in range(heads):
            put(hh, _rope_ret(head(hh), rc_ref[...], rs_ref[...]) * QK_SCALE)

    @pl.when((j == 2) | (j == 6))
    def _():
        o_ref[...] = acc_scr[...].astype(BF16)

    @pl.when(j == 3)
    def _():
        a = acc_scr[...]
        o_ref[...] = (a * _sigmoid(a)).astype(BF16)

    @pl.when(j == 4)
    def _():
        for hh in range(heads):
            put(hh, _rms(head(hh), gains_ref[0:1, :]))

    @pl.when(j == 5)
    def _():
        for hh in range(heads):
            put(hh, _rms(head(hh), gains_ref[1:2, :]))

    @pl.when((j == 7) | (j == 8))
    def _():
        for hh in range(heads):
            put(hh, _rope_axial(_rms(head(hh), gains_ref[2:3, :]), ac_ref[...], as_ref[...])
                * (QK_SCALE * LOG2E))

    @pl.when(j == 9)
    def _():
        for hh in range(2):
            put(hh, _rope_axial(_rms(head(hh), gains_ref[3:4, :]), ac_ref[...], as_ref[...]))
        o_ref[:, 2 * HEAD_DIM:] = acc_scr[:, 2 * HEAD_DIM:].astype(BF16)


def _inproj(x, mod5, layer, g1, w_in_bf, tabs, gains, geo):
    n = x.shape[0]
    tm = PROJ_TM

    def mod_map(chunk):
        def f(i, j):
            b, _ = geo.locate(i * tm)
            return (layer, chunk, b, 0, 0)
        return f

    def tab_map(i, j):
        _, pos = geo.locate(i * tm)
        return (pos // tm, 0)

    mod_spec = lambda chunk: pl.BlockSpec((None, None, None, 1, D_MODEL), mod_map(chunk))
    tab_spec = pl.BlockSpec((tm, HEAD_DIM), tab_map)
    return pl.pallas_call(
        _inproj_kernel,
        out_shape=jax.ShapeDtypeStruct((n, PROJ_W), BF16),
        grid=(n // tm, PROJ_W // PROJ_TN),
        in_specs=[
            pl.BlockSpec((tm, D_MODEL), lambda i, j: (i, 0)),
            mod_spec(1), mod_spec(0),
            pl.BlockSpec((1, D_MODEL), lambda i, j: (0, 0)),
            pl.BlockSpec((D_MODEL, PROJ_TN), lambda i, j: (0, j)),
            tab_spec, tab_spec, tab_spec, tab_spec,
            pl.BlockSpec((4, HEAD_DIM), lambda i, j: (0, 0)),
        ],
        out_specs=pl.BlockSpec((tm, PROJ_TN), lambda i, j: (i, j)),
        scratch_shapes=[pltpu.VMEM((tm, D_MODEL), BF16), pltpu.VMEM((tm, PROJ_TN), F32)],
        compiler_params=_cparams(("parallel", "arbitrary")),
        name="inproj",
    )(x, mod5, mod5, g1, w_in_bf, tabs[0], tabs[1], tabs[2], tabs[3], gains)


def _ret_kernel(lg_ref, q_ref, k_ref, v_ref, *rest, reverse, geo):
    if reverse:
        o_ref, s_scr, tab_scr = rest
    else:
        sg_ref, ob_ref, gn_ref, o_ref, s_scr, tab_scr = rest
    i = pl.program_id(0)
    nchunks = pl.num_programs(0)
    c = RET_CHUNK
    heads = RET_W // HEAD_DIM

    @pl.when(i == 0)
    def _():
        row = lax.broadcasted_iota(I32, (c, c), 0).astype(F32)
        col = lax.broadcasted_iota(I32, (c, c), 1).astype(F32)
        for hh in range(heads):
            lg = lg_ref[hh]
            if reverse:
                diff = col - row
                decay = jnp.where(diff > 0, jnp.exp(lg * jnp.maximum(diff, 0.0)), 0.0)
                qdec = jnp.exp(lg * (c - row))
                kdec = jnp.exp(lg * row)
            else:
                diff = row - col
                decay = jnp.where(diff >= 0, jnp.exp(lg * jnp.maximum(diff, 0.0)), 0.0)
                qdec = jnp.exp(lg * (row + 1.0))
                kdec = jnp.exp(lg * (c - 1.0 - row))
            tab_scr[hh, 0] = decay
            tab_scr[hh, 1] = qdec
            tab_scr[hh, 2] = kdec
            tab_scr[hh, 3] = jnp.exp(jnp.zeros((c, c), F32) + lg * c)

    chunk = (nchunks - 1 - i) if reverse else i
    row0 = chunk * c
    _, pos = geo.locate(row0)
    boundary = (pos + c == geo.seq_len(row0)) if reverse else (pos == 0)

    @pl.when(boundary)
    def _():
        s_scr[...] = jnp.zeros_like(s_scr)

    for hh in range(heads):
        sl = slice(hh * HEAD_DIM, (hh + 1) * HEAD_DIM)
        qh = q_ref[:, sl]
        kh = k_ref[:, sl]
        vh = v_ref[:, sl]
        s = _dot_nt(qh, kh) * tab_scr[hh, 0]
        o = _dot(s.astype(BF16), vh)
        qd = (qh.astype(F32) * tab_scr[hh, 1]).astype(BF16)
        o = o + _dot(qd, s_scr[hh].astype(BF16))
        kd_t = (kh.astype(F32) * tab_scr[hh, 2]).T.astype(BF16)
        s_scr[hh] = s_scr[hh] * tab_scr[hh, 3] + _dot(kd_t, vh)
        if reverse:
            o_ref[:, sl] = o
        else:
            tot = o + ob_ref[:, sl]
            y = _rms(tot, gn_ref[:, sl])
            o_ref[:, sl] = (sg_ref[:, sl].astype(F32) * y).astype(BF16)


def _retention(proj, lg_f, lg_b, ret_gn, geo):
    n = proj.shape[0]
    nchunks = n // RET_CHUNK
    c = RET_CHUNK
    heads = RET_W // HEAD_DIM
    scratch = [pltpu.VMEM((heads, HEAD_DIM, HEAD_DIM), F32), pltpu.VMEM((heads, 4, c, c), F32)]
    smem = pl.BlockSpec(memory_space=pltpu.SMEM)

    def col_spec(colblk, rev):
        if rev:
            return pl.BlockSpec((c, RET_W), lambda i: (nchunks - 1 - i, colblk))
        return pl.BlockSpec((c, RET_W), lambda i: (i, colblk))

    o_bwd = pl.pallas_call(
        functools.partial(_ret_kernel, reverse=True, geo=geo),
        out_shape=jax.ShapeDtypeStruct((n, RET_W), F32),
        grid=(nchunks,),
        in_specs=[smem, col_spec(0, True), col_spec(1, True), col_spec(2, True)],
        out_specs=pl.BlockSpec((c, RET_W), lambda i: (nchunks - 1 - i, 0)),
        scratch_shapes=scratch,
        compiler_params=_cparams(("arbitrary",)),
        name="ret_bwd",
    )(lg_b, proj, proj, proj)
    return pl.pallas_call(
        functools.partial(_ret_kernel, reverse=False, geo=geo),
        out_shape=jax.ShapeDtypeStruct((n, RET_W), BF16),
        grid=(nchunks,),
        in_specs=[smem, col_spec(0, False), col_spec(1, False), col_spec(2, False), col_spec(3, False),
                  pl.BlockSpec((c, RET_W), lambda i: (i, 0)),
                  pl.BlockSpec((1, RET_W), lambda i: (0, 0))],
        out_specs=pl.BlockSpec((c, RET_W), lambda i: (i, 0)),
        scratch_shapes=scratch,
        compiler_params=_cparams(("arbitrary",)),
        name="ret_fwd",
    )(lg_f, proj, proj, proj, proj, o_bwd, ret_gn)


def _na_geometry(i, geo):
    rp, rs = geo.tp // GRID_W, geo.ts // GRID_W
    r0g = i * NA_QROWS
    in_p = r0g < geo.bp * rp
    rsmp = jnp.maximum(r0g - geo.bp * rp, 0)
    rows = jnp.where(in_p, rp, rs)
    lr0 = jnp.where(in_p, r0g % rp, rsmp % rs)
    slab = jnp.clip(lr0 - NA_WIN_R // 2, 0, rows - NA_SLAB_ROWS)
    return rows, lr0, slab, r0g - lr0


def _na_kernel(q_ref, k0, k1, k2, k3, v0, v1, v2, v3, tab_ref, on_ref, o_ref, ks_scr, vs_scr, *, geo):
    i = pl.program_id(0)
    rows, lr0, slab, _ = _na_geometry(i, geo)
    blk = NA_KBLK_ROWS * GRID_W
    for m, (kr, vr) in enumerate(((k0, v0), (k1, v1), (k2, v2), (k3, v3))):
        ks_scr[m * blk:(m + 1) * blk, :] = kr[...]
        vs_scr[m * blk:(m + 1) * blk, :] = vr[...]
    heads = NA_W // HEAD_DIM
    win = NA_WIN_R * GRID_W

    def body(p, carry):
        r = lr0 + p
        rs = jnp.clip(r - NA_WIN_R // 2, 0, rows - NA_WIN_R)
        off = pl.multiple_of((rs - slab) * GRID_W, GRID_W)
        var = r - rs
        qrow = q_ref[pl.ds(pl.multiple_of(p * GRID_W, GRID_W), GRID_W), :]
        outs = []
        for hh in range(heads):
            sl = slice(hh * HEAD_DIM, (hh + 1) * HEAD_DIM)
            kw = ks_scr[pl.ds(off, win), sl]
            vw = vs_scr[pl.ds(off, win), sl]
            s = _dot_nt(qrow[:, sl], kw) * QK_SCALE + tab_ref[hh, var]
            e = jnp.exp(s - jnp.max(s, axis=-1, keepdims=True))
            prob = e / jnp.sum(e, axis=-1, keepdims=True)
            outs.append(_dot(prob.astype(BF16), vw))
        o = jnp.concatenate(outs, axis=-1)
        o_ref[pl.ds(pl.multiple_of(p * GRID_W, GRID_W), GRID_W), :] = _rms(o, on_ref[...]).astype(BF16)
        return carry

    lax.fori_loop(0, NA_QROWS, body, 0)


def _na_bias_table(rpb):
    rpb = rpb.astype(F32)
    rows = jnp.stack([rpb[:, NA_WIN_R - 1 - v:2 * NA_WIN_R - 1 - v, :] for v in range(NA_WIN_R)], axis=1)
    width = 2 * GRID_W
    lead = GRID_W - NA_WIN_C
    p = jnp.pad(rows, ((0, 0), (0, 0), (0, 0), (lead, width - lead - (2 * NA_WIN_C - 1))))
    skew = jnp.tile(p, GRID_W)[..., :GRID_W * (width - 1)].reshape(p.shape[:-1] + (GRID_W, width - 1))
    toe = skew[..., GRID_W - 1:]
    c = np.arange(GRID_W)[:, None]
    j = np.arange(GRID_W)[None, :]
    cs = np.clip(c - NA_WIN_C // 2, 0, GRID_W - NA_WIN_C)
    valid = (j >= cs) & (j < cs + NA_WIN_C)
    tab = jnp.where(valid[None, None, :, None, :], toe.transpose(0, 1, 3, 2, 4), MASK_VALUE)
    return tab.reshape(rpb.shape[0], NA_WIN_R, GRID_W, NA_WIN_R * GRID_W)


def _neighbourhood(proj, bias_tab, na_on, geo):
    n = proj.shape[0]
    tq = NA_QROWS * GRID_W
    blk = NA_KBLK_ROWS * GRID_W
    nblk = NA_SLAB_ROWS // NA_KBLK_ROWS
    qcol, kcol, vcol = 4, 5, 6

    def kv_spec(colblk, m):
        def f(i):
            _, _, slab, seq_row0 = _na_geometry(i, geo)
            return ((seq_row0 + slab) // NA_KBLK_ROWS + m, colblk)
        return pl.BlockSpec((blk, NA_W), f)

    heads = NA_W // HEAD_DIM
    return pl.pallas_call(
        functools.partial(_na_kernel, geo=geo),
        out_shape=jax.ShapeDtypeStruct((n, NA_W), BF16),
        grid=(n // tq,),
        in_specs=[pl.BlockSpec((tq, NA_W), lambda i: (i, qcol))]
        + [kv_spec(kcol, m) for m in range(nblk)]
        + [kv_spec(vcol, m) for m in range(nblk)]
        + [pl.BlockSpec((heads, NA_WIN_R, GRID_W, NA_WIN_R * GRID_W), lambda i: (0, 0, 0, 0)),
           pl.BlockSpec((1, NA_W), lambda i: (0, 0))],
        out_specs=pl.BlockSpec((tq, NA_W), lambda i: (i, 0)),
        scratch_shapes=[pltpu.VMEM((NA_SLAB_ROWS * GRID_W, NA_W), BF16),
                        pltpu.VMEM((NA_SLAB_ROWS * GRID_W, NA_W), BF16)],
        compiler_params=_cparams(("parallel",)),
        name="natten",
    )(*([proj] * (1 + 2 * nblk)), bias_tab, na_on)


def _chunked(fn, x, rows):
    return jnp.concatenate([fn(x[c * rows:(c + 1) * rows]) for c in range(x.shape[0] // rows)], axis=0)


def _flash_kernel(q_ref, k_ref, v_ref, o_ref, m_scr, acc_scr, va_scr):
    ki = pl.program_id(3)
    groups = q_ref.shape[1] // HEAD_DIM
    tk = k_ref.shape[0]

    @pl.when(ki == 0)
    def _():
        m_scr[...] = jnp.full_like(m_scr, NEG_INF)
        acc_scr[...] = jnp.zeros_like(acc_scr)
        lane = lax.broadcasted_iota(I32, (tk, HEAD_DIM), 1)
        va_scr[:, HEAD_DIM:] = jnp.where(lane == 0, 1.0, 0.0).astype(BF16)

    va_scr[:, :HEAD_DIM] = v_ref[...]
    k = k_ref[...]
    va = va_scr[...]
    for gq in range(groups):
        q = q_ref[:, gq * HEAD_DIM:(gq + 1) * HEAD_DIM]
        s = _chunked(lambda x: _dot_nt(x, k), q, FLASH_DOT_ROWS)
        m_prev = m_scr[gq][:, :1]
        m_next = jnp.maximum(m_prev, jnp.max(s, axis=1, keepdims=True))
        alpha = jnp.exp2(m_prev - m_next)
        p = jnp.exp2(s - m_next).astype(BF16)
        acc_scr[gq] = alpha * acc_scr[gq] + _chunked(lambda x: _dot(x, va), p, FLASH_DOT_ROWS)
        m_scr[gq] = jnp.broadcast_to(m_next, m_scr.shape[1:])

    @pl.when(ki == pl.num_programs(3) - 1)
    def _():
        for gq in range(groups):
            acc = acc_scr[gq]
            o_ref[:, gq * HEAD_DIM:(gq + 1) * HEAD_DIM] = (
                acc[:, :HEAD_DIM] / acc[:, HEAD_DIM:HEAD_DIM + 1]).astype(BF16)


def _flash_group(proj, row0, batch, t):
    tq, tk = min(FLASH_TQ, t), min(FLASH_TK, t)
    assert row0 % tq == 0 and row0 % tk == 0 and t % tq == 0 and t % tk == 0
    kv_heads = GA_KV_W // HEAD_DIM
    qw = GA_W // kv_heads
    groups = qw // HEAD_DIM
    qcol0 = (4 * RET_W + 3 * NA_W) // qw
    kcol0 = (4 * RET_W + 3 * NA_W + GA_W) // HEAD_DIM
    vcol0 = kcol0 + kv_heads
    return pl.pallas_call(
        _flash_kernel,
        out_shape=jax.ShapeDtypeStruct((batch * t, GA_W), BF16),
        grid=(batch, kv_heads, t // tq, t // tk),
        in_specs=[
            pl.BlockSpec((tq, qw), lambda b, h, qi, ki: ((row0 + b * t) // tq + qi, qcol0 + h)),
            pl.BlockSpec((tk, HEAD_DIM), lambda b, h, qi, ki: ((row0 + b * t) // tk + ki, kcol0 + h)),
            pl.BlockSpec((tk, HEAD_DIM), lambda b, h, qi, ki: ((row0 + b * t) // tk + ki, vcol0 + h)),
        ],
        out_specs=pl.BlockSpec((tq, qw), lambda b, h, qi, ki: ((b * t) // tq + qi, h)),
        scratch_shapes=[pltpu.VMEM((groups, tq, HEAD_DIM), F32),
                        pltpu.VMEM((groups, tq, 2 * HEAD_DIM), F32),
                        pltpu.VMEM((tk, 2 * HEAD_DIM), BF16)],
        compiler_params=_cparams(("parallel", "parallel", "parallel", "arbitrary")),
        name="flash_gqa",
    )(proj, proj, proj)


def _outproj_kernel(yr_ref, yn_ref, yg_ref, x_ref, w_ref, gon_ref, gt_ref, sc_ref, sh_ref, g2_ref,
                    x1_ref, h2_ref):
    ygn = _rms(yg_ref[...].astype(F32), gon_ref[...]).astype(BF16)
    acc = _dot(yr_ref[...], w_ref[0:RET_W, :])
    acc = acc + _dot(yn_ref[...], w_ref[RET_W:RET_W + NA_W, :])
    acc = acc + _dot(ygn, w_ref[RET_W + NA_W:, :])
    x1 = x_ref[...] + gt_ref[...] * acc
    x1_ref[...] = x1
    h2_ref[...] = _rms(x1, g2_ref[...]) * (1.0 + sc_ref[...]) + sh_ref[...]


def _outproj(y_ret, y_na, y_ga, x, w_out_bf, ga_on, mod5, layer, g2, geo):
    n = x.shape[0]
    tm = OUT_TM

    def mod_spec(chunk):
        def f(i):
            b, _ = geo.locate(i * tm)
            return (layer, chunk, b, 0, 0)
        return pl.BlockSpec((None, None, None, 1, D_MODEL), f)

    row = lambda w: pl.BlockSpec((tm, w), lambda i: (i, 0))
    return pl.pallas_call(
        _outproj_kernel,
        out_shape=(jax.ShapeDtypeStruct((n, D_MODEL), F32), jax.ShapeDtypeStruct((n, D_MODEL), F32)),
        grid=(n // tm,),
        in_specs=[row(RET_W), row(NA_W), row(GA_W), row(D_MODEL),
                  pl.BlockSpec((D_MODEL, D_MODEL), lambda i: (0, 0)),
                  pl.BlockSpec((1, GA_W), lambda i: (0, 0)),
                  mod_spec(2), mod_spec(4), mod_spec(3),
                  pl.BlockSpec((1, D_MODEL), lambda i: (0, 0))],
        out_specs=(row(D_MODEL), row(D_MODEL)),
        compiler_params=_cparams(("parallel",)),
        name="outproj",
    )(y_ret, y_na, y_ga, x, w_out_bf, ga_on, mod5, mod5, mod5, g2)


def _first_index_of_max(vals, ids, axes, sentinel):
    m = vals
    for ax in axes:
        m = jnp.max(m, axis=ax, keepdims=True)
    cand = jnp.where(vals == m, ids, sentinel)
    for ax in axes:
        cand = jnp.min(cand, axis=ax, keepdims=True)
    return m, cand


def _router_kernel(h_ref, w_ref, b_ref, idx_ref, wgt_ref, rank_ref, cnt_ref, cnt_scr, tri_scr):
    i = pl.program_id(0)
    tm = h_ref.shape[0]

    @pl.when(i == 0)
    def _():
        cnt_scr[...] = jnp.zeros_like(cnt_scr)
        r = lax.broadcasted_iota(I32, (tm, tm), 0)
        c = lax.broadcasted_iota(I32, (tm, tm), 1)
        tri_scr[...] = jnp.where(r < c, 1.0, 0.0).astype(BF16)

    logits = lax.dot_general(w_ref[...], h_ref[...], (((1,), (1,)), ((), ())),
                             preferred_element_type=F32, precision=lax.Precision.HIGHEST)
    scores = _sigmoid(logits)
    choice = scores + b_ref[...][:, :1]
    shape3 = (N_GROUPS, GROUP_SIZE, tm)
    choice3 = choice.reshape(shape3)
    scores3 = scores.reshape(shape3)
    sub = lax.broadcasted_iota(I32, shape3, 1)
    grp = lax.broadcasted_iota(I32, shape3, 0)
    eid = grp * GROUP_SIZE + sub

    m1, i1 = _first_index_of_max(choice3, sub, (1,), GROUP_SIZE)
    rest = jnp.where(sub == i1, NEG_INF, choice3)
    m2 = jnp.max(rest, axis=1, keepdims=True)
    gscore = m1 + m2

    gid = lax.broadcasted_iota(I32, gscore.shape, 0)
    gsel = jnp.zeros(gscore.shape, F32)
    for _ in range(TOPK_GROUPS):
        _, gi = _first_index_of_max(gscore, gid, (0,), N_GROUPS)
        hit = gid == gi
        gsel = jnp.where(hit, 1.0, gsel)
        gscore = jnp.where(hit, NEG_INF, gscore)

    masked = jnp.where(gsel > 0.0, choice3, NEG_INF)
    onehots, ids, ws = [], [], []
    for _ in range(TOP_K):
        _, ei = _first_index_of_max(masked, eid, (1, 0), N_EXPERTS)
        hit = eid == ei
        onehots.append(hit)
        ids.append(ei.reshape(1, tm))
        ws.append(jnp.sum(jnp.where(hit, scores3, 0.0), axis=(0, 1), keepdims=True).reshape(1, tm))
        masked = jnp.where(hit, NEG_INF, masked)
    wsum = ws[0]
    for k in range(1, TOP_K):
        wsum = wsum + ws[k]

    sel = jnp.zeros(shape3, F32)
    for hit in onehots:
        sel = jnp.where(hit, 1.0, sel)
    sel2 = sel.reshape(N_EXPERTS, tm)
    before = _dot(sel2.astype(BF16), tri_scr[...]) + cnt_scr[:, :1]
    before3 = before.reshape(shape3)
    ranks = [jnp.sum(jnp.where(hit, before3, 0.0), axis=(0, 1), keepdims=True).reshape(1, tm)
             for hit in onehots]
    cnt_scr[...] = cnt_scr[...] + jnp.sum(sel2, axis=1, keepdims=True)

    idx_ref[...] = jnp.concatenate(ids, axis=0)
    wgt_ref[...] = jnp.concatenate([w / wsum * ROUTED_SCALE for w in ws], axis=0)
    rank_ref[...] = jnp.concatenate(ranks, axis=0).astype(I32)
    cnt_ref[...] = cnt_scr[...]


def _router(h2, w_router_t, bias_col):
    n = h2.shape[0]
    tm = ROUTER_TM
    out_blk = pl.BlockSpec((TOP_K, tm), lambda i: (0, i))
    return pl.pallas_call(
        _router_kernel,
        out_shape=(jax.ShapeDtypeStruct((TOP_K, n), I32), jax.ShapeDtypeStruct((TOP_K, n), F32),
                   jax.ShapeDtypeStruct((TOP_K, n), I32), jax.ShapeDtypeStruct((N_EXPERTS, HEAD_DIM), F32)),
        grid=(n // tm,),
        in_specs=[pl.BlockSpec((tm, D_MODEL), lambda i: (i, 0)),
                  pl.BlockSpec((N_EXPERTS, D_MODEL), lambda i: (0, 0)),
                  pl.BlockSpec((N_EXPERTS, HEAD_DIM), lambda i: (0, 0))],
        out_specs=(out_blk, out_blk, out_blk, pl.BlockSpec((N_EXPERTS, HEAD_DIM), lambda i: (0, 0))),
        scratch_shapes=[pltpu.VMEM((N_EXPERTS, HEAD_DIM), F32), pltpu.VMEM((tm, tm), BF16)],
        compiler_params=_cparams(("arbitrary",)),
        name="router",
    )(h2, w_router_t, bias_col)


def _row_copy(src, src_row, dst, dst_row, sem, rows=1):
    return pltpu.make_async_copy(src.at[pl.ds(src_row, rows)], dst.at[pl.ds(dst_row, rows)], sem)


def _zero_fill_padding(fill_ref, end_ref, xs_hbm, zero_scr, zsem, wait):
    def go(copy):
        copy.wait() if wait else copy.start()

    def per_expert(e, carry):
        def per_row(r, c):
            go(pltpu.make_async_copy(zero_scr.at[pl.ds(0, 1)], xs_hbm.at[pl.ds(r, 1)], zsem))
            return c
        return lax.fori_loop(fill_ref[e], end_ref[e], per_row, carry)

    lax.fori_loop(0, N_EXPERTS, per_expert, 0)
    total = end_ref[N_EXPERTS - 1]
    n_tiles = xs_hbm.shape[0] // FFN_TM
    for tile in range(n_tiles - N_EXPERTS, n_tiles):
        @pl.when(tile * FFN_TM >= total)
        def _(tile=tile):
            go(pltpu.make_async_copy(zero_scr, xs_hbm.at[pl.ds(tile * FFN_TM, FFN_TM)], zsem))


def _dispatch_kernel(fill_ref, end_ref, pos_ref, h_ref, xs_hbm, zero_scr, sem, zsem):
    i = pl.program_id(0)
    tm = pos_ref.shape[1]

    @pl.when(i == 0)
    def _():
        zero_scr[...] = jnp.zeros_like(zero_scr)
        _zero_fill_padding(fill_ref, end_ref, xs_hbm, zero_scr, zsem, wait=False)
        _zero_fill_padding(fill_ref, end_ref, xs_hbm, zero_scr, zsem, wait=True)

    def issue(t, carry):
        for k in range(TOP_K):
            _row_copy(h_ref, t, xs_hbm, pos_ref[k, t], sem).start()
        return carry

    lax.fori_loop(0, tm, issue, 0)
    def drain(t, carry):
        _row_copy(h_ref, 0, xs_hbm, 0, sem, rows=TOP_K).wait()
        return carry

    lax.fori_loop(0, tm, drain, 0)


def _dispatch(h2, pos, seg_fill, seg_end, n_slots):
    n = h2.shape[0]
    tm = DISPATCH_TM
    smem_blk = pl.BlockSpec((TOP_K, tm), lambda i, *_: (0, i), memory_space=pltpu.SMEM)
    return pl.pallas_call(
        _dispatch_kernel,
        out_shape=jax.ShapeDtypeStruct((n_slots, D_MODEL), F32),
        grid_spec=pltpu.PrefetchScalarGridSpec(
            num_scalar_prefetch=2,
            grid=(n // tm,),
            in_specs=[smem_blk, pl.BlockSpec((tm, D_MODEL), lambda i, *_: (i, 0))],
            out_specs=pl.BlockSpec(memory_space=pl.ANY),
            scratch_shapes=[pltpu.VMEM((FFN_TM, D_MODEL), F32),
                            pltpu.SemaphoreType.DMA(()), pltpu.SemaphoreType.DMA(())],
        ),
        compiler_params=_cparams(("arbitrary",)),
        name="moe_dispatch",
    )(seg_fill, seg_end, pos, h2)


def _ffn_kernel(te_ref, nv_ref, x_ref, wg_ref, wu_ref, wd_ref, o_ref):
    i = pl.program_id(0)
    nvalid = nv_ref[i]

    @pl.when(nvalid > 0)
    def _():
        x = x_ref[...].astype(BF16)
        hg = _dot(x, wg_ref[...])
        hu = _dot(x, wu_ref[...])
        h = (hg * _sigmoid(hg) * hu).astype(BF16)
        o_ref[...] = _dot(h, wd_ref[...])

    @pl.when(nvalid == 0)
    def _():
        o_ref[...] = jnp.zeros_like(o_ref)


def _expert_ffn(x_sorted, tile_expert, tile_valid, wg, wu, wd):
    n_slots = x_sorted.shape[0]
    tm = FFN_TM
    return pl.pallas_call(
        _ffn_kernel,
        out_shape=jax.ShapeDtypeStruct((n_slots, D_MODEL), F32),
        grid_spec=pltpu.PrefetchScalarGridSpec(
            num_scalar_prefetch=2,
            grid=(n_slots // tm,),
            in_specs=[pl.BlockSpec((tm, D_MODEL), lambda i, te, nv: (i, 0)),
                      pl.BlockSpec((None, D_MODEL, EXPERT_FF), lambda i, te, nv: (te[i], 0, 0)),
                      pl.BlockSpec((None, D_MODEL, EXPERT_FF), lambda i, te, nv: (te[i], 0, 0)),
                      pl.BlockSpec((None, EXPERT_FF, D_MODEL), lambda i, te, nv: (te[i], 0, 0))],
            out_specs=pl.BlockSpec((tm, D_MODEL), lambda i, te, nv: (i, 0)),
        ),
        compiler_params=_cparams(("parallel",)),
        name="moe_ffn",
    )(tile_expert, tile_valid, x_sorted, wg, wu, wd)


def _combine_kernel(pos_ref, posn_ref, wgt_ref, x1_ref, h_ref, gt_ref,
                    wsg_ref, wsu_ref, wsd_ref, y_hbm, o_ref, ybuf, sems):
    i = pl.program_id(0)
    nsteps = pl.num_programs(0)
    tm = x1_ref.shape[0]

    def issue(slot, pr):
        def body(t, carry):
            for k in range(TOP_K):
                pltpu.make_async_copy(y_hbm.at[pl.ds(pr[k, t], 1)], ybuf.at[slot, k, pl.ds(t, 1)],
                                      sems.at[slot]).start()
            return carry
        lax.fori_loop(0, tm, body, 0)

    slot = i % 2

    @pl.when(i == 0)
    def _():
        issue(0, pos_ref)

    @pl.when(i + 1 < nsteps)
    def _():
        issue(1 - slot, posn_ref)

    hb = h_ref[...].astype(BF16)
    hg = _dot(hb, wsg_ref[...])
    hu = _dot(hb, wsu_ref[...])
    shared = _dot((hg * _sigmoid(hg) * hu).astype(BF16), wsd_ref[...])

    for k in range(TOP_K):
        pltpu.make_async_copy(y_hbm.at[pl.ds(0, tm)], ybuf.at[slot, k], sems.at[slot]).wait()

    acc = shared
    for k in range(TOP_K):
        acc = acc + wgt_ref[:, k:k + 1] * ybuf[slot, k]
    o_ref[...] = x1_ref[...] + gt_ref[...] * acc


def _combine(y_sorted, pos, wgt_t, x1, h2, mod5, layer, wsg, wsu, wsd, geo):
    n = x1.shape[0]
    tm = COMBINE_TM
    nsteps = n // tm
    smem_cur = pl.BlockSpec((TOP_K, tm), lambda i: (0, i), memory_space=pltpu.SMEM)
    smem_next = pl.BlockSpec((TOP_K, tm), lambda i: (0, jnp.minimum(i + 1, nsteps - 1)),
                             memory_space=pltpu.SMEM)

    def gt_map(i):
        b, _ = geo.locate(i * tm)
        return (layer, 5, b, 0, 0)

    row = pl.BlockSpec((tm, D_MODEL), lambda i: (i, 0))
    whole = lambda a, b: pl.BlockSpec((a, b), lambda i: (0, 0))
    return pl.pallas_call(
        _combine_kernel,
        out_shape=jax.ShapeDtypeStruct((n, D_MODEL), F32),
        grid=(nsteps,),
        in_specs=[smem_cur, smem_next,
                  pl.BlockSpec((tm, TOP_K), lambda i: (i, 0)),
                  row, row,
                  pl.BlockSpec((None, None, None, 1, D_MODEL), gt_map),
                  whole(D_MODEL, EXPERT_FF), whole(D_MODEL, EXPERT_FF), whole(EXPERT_FF, D_MODEL),
                  pl.BlockSpec(memory_space=pl.ANY)],
        out_specs=row,
        scratch_shapes=[pltpu.VMEM((2, TOP_K, tm, D_MODEL), F32), pltpu.SemaphoreType.DMA((2,))],
        compiler_params=_cparams(("arbitrary",)),
        name="moe_combine",
    )(pos, pos, wgt_t, x1, h2, mod5, wsg, wsu, wsd, y_sorted)


def _rope_tables(t_max):
    pos = jnp.arange(t_max, dtype=jnp.int32)

    def angles(p, n_pairs):
        inv = ROPE_BASE ** (-jnp.arange(n_pairs, dtype=F32) / n_pairs)
        ang = p.astype(F32)[:, None] * inv[None, :]
        return jnp.cos(ang), jnp.sin(ang)

    c, s = angles(pos, HEAD_DIM // 2)
    cr, sr = angles(pos // GRID_W, HEAD_DIM // 4)
    cc, sc = angles(pos % GRID_W, HEAD_DIM // 4)
    return (jnp.concatenate([c, c], -1), jnp.concatenate([-s, s], -1),
            jnp.concatenate([cr, cr, cc, cc], -1), jnp.concatenate([-sr, sr, -sc, sc], -1))


def _moe_plan(counts, n_slots):
    padded = (counts + FFN_TM - 1) // FFN_TM * FFN_TM
    seg_end = jnp.cumsum(padded)
    seg_start = seg_end - padded
    tile_row0 = jnp.arange(n_slots // FFN_TM, dtype=I32) * FFN_TM
    tile_expert = jnp.minimum(jnp.sum(tile_row0[:, None] >= seg_end[None, :], axis=1), N_EXPERTS - 1).astype(I32)
    tile_valid = jnp.clip(seg_start[tile_expert] + counts[tile_expert] - tile_row0, 0, FFN_TM).astype(I32)
    return (seg_start.astype(I32), (seg_start + counts).astype(I32), seg_end.astype(I32),
            tile_expert, tile_valid)


def _layer(x, mod5, layer, p, tabs, geo):
    n = x.shape[0]
    proj = _inproj(x, mod5, layer, p["g1"], p["w_in"], tabs, p["gains"], geo)
    y_ret = _retention(proj, p["lg_f"], p["lg_b"], p["ret_gn"], geo)
    y_na = _neighbourhood(proj, p["na_tab"], p["na_on"], geo)
    y_ga = jnp.concatenate([_flash_group(proj, 0, geo.bp, geo.tp),
                            _flash_group(proj, geo.n_p, geo.bs, geo.ts)], axis=0)
    x1, h2 = _outproj(y_ret, y_na, y_ga, x, p["w_out"], p["ga_on"], mod5, layer, p["g2"], geo)
    idx, wgt, rank, cnt = _router(h2, p["w_router_t"], p["router_bias"])
    n_slots = n * TOP_K + N_EXPERTS * FFN_TM
    seg_start, seg_fill, seg_end, tile_expert, tile_valid = _moe_plan(cnt[:, 0].astype(I32), n_slots)
    experts = jnp.arange(N_EXPERTS, dtype=I32)
    pos = rank + jnp.sum(jnp.where(idx[:, :, None] == experts, seg_start, 0), axis=-1)
    x_sorted = _dispatch(h2, pos, seg_fill, seg_end, n_slots)
    y_sorted = _expert_ffn(x_sorted, tile_expert, tile_valid, p["w_eg"], p["w_eu"], p["w_ed"])
    return _combine(y_sorted, pos, wgt.T, x1, h2, mod5, layer, p["w_sg"], p["w_su"], p["w_sd"], geo)


def kernel(x_prompt, x_sample, c_prompt, c_sample, w_ada, b_ada, norm1, w_in, ret_decay_fwd, ret_decay_bwd, ret_norm, na_q_norm, na_k_norm, na_rpb, na_out_norm, ga_q_norm, ga_k_norm, ga_out_norm, w_out, norm2, w_router, router_bias, w_exp_gate, w_exp_up, w_exp_down, w_sh_gate, w_sh_up, w_sh_down):
    bp, tp, d = x_prompt.shape
    bs, ts, _ = x_sample.shape
    depth = w_ada.shape[0]
    geo = Geom(bp, tp, bs, ts)
    assert d == D_MODEL and w_in.shape[-1] == PROJ_W
    for t in (tp, ts):
        assert t % max(PROJ_TM, FLASH_TQ, NA_SLAB_ROWS * GRID_W) == 0 and t % min(FLASH_TK, t) == 0

    x = jnp.concatenate([x_prompt.reshape(bp * tp, d), x_sample.reshape(bs * ts, d)], axis=0)
    c_all = jnp.concatenate([c_prompt, c_sample], axis=0)
    c_pad = jnp.zeros((8 * pl.cdiv(geo.nb, 8), d), F32).at[:geo.nb].set(c_all)
    mod = _adaln(c_pad, w_ada, b_ada)
    rows = mod.shape[1]
    mod5 = mod.reshape(depth, rows, N_MOD, 1, d).transpose(0, 2, 1, 3, 4)
    tabs = _rope_tables(max(tp, ts))

    for l in range(depth):
        p = {
            "g1": norm1[l].reshape(1, d),
            "w_in": w_in[l].astype(BF16),
            "gains": jnp.stack([na_q_norm[l], na_k_norm[l], ga_q_norm[l], ga_k_norm[l]]),
            "lg_f": jax.nn.log_sigmoid(ret_decay_fwd[l].astype(F32)),
            "lg_b": jax.nn.log_sigmoid(ret_decay_bwd[l].astype(F32)),
            "ret_gn": ret_norm[l].reshape(1, RET_W),
            "na_tab": _na_bias_table(na_rpb[l]),
            "na_on": na_out_norm[l].reshape(1, NA_W),
            "ga_on": ga_out_norm[l].reshape(1, GA_W),
            "w_out": w_out[l].astype(BF16),
            "g2": norm2[l].reshape(1, d),
            "w_router_t": w_router[l].astype(F32).T,
            "router_bias": jnp.broadcast_to(router_bias[l].astype(F32)[:, None], (N_EXPERTS, HEAD_DIM)),
            "w_eg": w_exp_gate[l].astype(BF16),
            "w_eu": w_exp_up[l].astype(BF16),
            "w_ed": w_exp_down[l].astype(BF16),
            "w_sg": w_sh_gate[l].astype(BF16),
            "w_su": w_sh_up[l].astype(BF16),
            "w_sd": w_sh_down[l].astype(BF16),
        }
        x = _layer(x, mod5, l, p, tabs, geo)

    y_prompt = x[:geo.n_p].reshape(bp, tp, d)
    y_sample = x[geo.n_p:].reshape(bs, ts, d)
    return (y_prompt, y_sample)
```

```python
import dataclasses
import functools

import numpy as np
import jax
import jax.numpy as jnp
from jax import lax
from jax.experimental import pallas as pl
from jax.experimental.pallas import tpu as pltpu

F32 = jnp.float32
BF16 = jnp.bfloat16
I32 = jnp.int32

D_MODEL = 2048
HEAD_DIM = 128
RET_W = 512
NA_W = 512
GA_W = 1024
GA_KV_W = 256
PROJ_W = 4 * RET_W + 3 * NA_W + GA_W + 2 * GA_KV_W
RET_CHUNK = 128
GRID_W = 64
NA_WIN_R = 8
NA_WIN_C = 16
ROPE_BASE = 10000.0
N_EXPERTS = 64
N_GROUPS = 8
GROUP_SIZE = N_EXPERTS // N_GROUPS
TOPK_GROUPS = 4
TOP_K = 8
EXPERT_FF = 512
ROUTED_SCALE = 2.5
N_MOD = 6
EPS = 1e-6
QK_SCALE = HEAD_DIM ** -0.5
LOG2E = 1.4426950408889634
NEG_INF = float("-inf")
MASK_VALUE = -1e30

VMEM_LIMIT_BYTES = 56 * 1024 * 1024

PROJ_TN = 512
PROJ_TM = 1024
NA_QROWS = 8
NA_SLAB_ROWS = 16
NA_KBLK_ROWS = 4
NA_DOT_ROWS = 256
FLASH_TQ = 1024
FLASH_TK = 4096
FLASH_DOT_ROWS = 256
OUT_TM = 256
ROUTER_TM = 512
DISPATCH_TM = 256
FFN_TM = 512
COMBINE_TM = 128
ADALN_TN = 1024


@dataclasses.dataclass(frozen=True)
class Geom:
    bp: int
    tp: int
    bs: int
    ts: int

    @property
    def n_p(self):
        return self.bp * self.tp

    @property
    def n(self):
        return self.bp * self.tp + self.bs * self.ts

    @property
    def nb(self):
        return self.bp + self.bs

    def locate(self, row):
        in_p = row < self.n_p
        rs = jnp.maximum(row - self.n_p, 0)
        b = jnp.where(in_p, row // self.tp, self.bp + rs // self.ts)
        pos = jnp.where(in_p, row % self.tp, rs % self.ts)
        return b, pos

    def seq_len(self, row):
        return jnp.where(row < self.n_p, self.tp, self.ts)


def _cparams(sem, vmem=VMEM_LIMIT_BYTES):
    return pltpu.CompilerParams(dimension_semantics=sem, vmem_limit_bytes=vmem)


def _dot(a, b):
    return jnp.dot(a, b, preferred_element_type=F32)


def _dot_nt(a, b):
    return lax.dot_general(a, b, (((1,), (1,)), ((), ())), preferred_element_type=F32)


def _sigmoid(x):
    return 1.0 / (1.0 + jnp.exp(-x))


def _rms(x, g):
    return x * lax.rsqrt(jnp.mean(x * x, axis=-1, keepdims=True) + EPS) * g


def _adaln_kernel(c_ref, w_ref, b_ref, o_ref):
    c = c_ref[...]
    s = c * _sigmoid(c)
    o_ref[...] = jnp.dot(s, w_ref[...], preferred_element_type=F32,
                         precision=lax.Precision.HIGHEST) + b_ref[...]


def _adaln(c_pad, w_ada, b_ada):
    depth, d, w = w_ada.shape
    rows = c_pad.shape[0]
    return pl.pallas_call(
        _adaln_kernel,
        out_shape=jax.ShapeDtypeStruct((depth, rows, w), F32),
        grid=(depth, w // ADALN_TN),
        in_specs=[
            pl.BlockSpec((rows, d), lambda l, j: (0, 0)),
            pl.BlockSpec((None, d, ADALN_TN), lambda l, j: (l, 0, j)),
            pl.BlockSpec((None, 1, ADALN_TN), lambda l, j: (l, 0, j)),
        ],
        out_specs=pl.BlockSpec((None, rows, ADALN_TN), lambda l, j: (l, 0, j)),
        compiler_params=_cparams(("parallel", "parallel")),
        name="adaln",
    )(c_pad, w_ada, b_ada.reshape(depth, 1, w))


def _rope_ret(x, cos, sin):
    return x * cos + pltpu.roll(x, HEAD_DIM // 2, 1) * sin


def _rope_axial(x, cos, sin):
    lane = lax.broadcasted_iota(I32, x.shape, 1)
    first = (lane & (HEAD_DIM // 4)) == 0
    rot = jnp.where(first, pltpu.roll(x, HEAD_DIM - HEAD_DIM // 4, 1), pltpu.roll(x, HEAD_DIM // 4, 1))
    return x * cos + rot * sin


def _inproj_kernel(x_ref, sc_ref, sh_ref, g1_ref, w_ref, rc_ref, rs_ref, ac_ref, as_ref, gains_ref,
                   o_ref, h_scr, acc_scr):
    j = pl.program_id(1)

    @pl.when(j == 0)
    def _():
        h = _rms(x_ref[...], g1_ref[...]) * (1.0 + sc_ref[...]) + sh_ref[...]
        h_scr[...] = h.astype(BF16)

    acc_scr[...] = _dot(h_scr[...], w_ref[...])
    heads = PROJ_TN // HEAD_DIM

    def head(hh):
        return acc_scr[:, hh * HEAD_DIM:(hh + 1) * HEAD_DIM]

    def put(hh, val):
        o_ref[:, hh * HEAD_DIM:(hh + 1) * HEAD_DIM] = val.astype(BF16)

    @pl.when(j == 0)
    def _():
        for hh in range(heads):
            put(hh, _rope_ret(head(hh), rc_ref[...], rs_ref[...]))

    @pl.when(j == 1)
    def _():
        for hh in range(heads):
            put(hh, _rope_ret(head(hh), rc_ref[...], rs_ref[...]) * QK_SCALE)

    @pl.when((j == 2) | (j == 6))
    def _():
        o_ref[...] = acc_scr[...].astype(BF16)

    @pl.when(j == 3)
    def _():
        a = acc_scr[...]
        o_ref[...] = (a * _sigmoid(a)).astype(BF16)

    @pl.when(j == 4)
    def _():
        for hh in range(heads):
            put(hh, _rms(head(hh), gains_ref[0:1, :]))

    @pl.when(j == 5)
    def _():
        for hh in range(heads):
            put(hh, _rms(head(hh), gains_ref[1:2, :]))

    @pl.when((j == 7) | (j == 8))
    def _():
        for hh in range(heads):
            put(hh, _rope_axial(_rms(head(hh), gains_ref[2:3, :]), ac_ref[...], as_ref[...])
                * (QK_SCALE * LOG2E))

    @pl.when(j == 9)
    def _():
        for hh in range(2):
            put(hh, _rope_axial(_rms(head(hh), gains_ref[3:4, :]), ac_ref[...], as_ref[...]))
        o_ref[:, 2 * HEAD_DIM:] = acc_scr[:, 2 * HEAD_DIM:].astype(BF16)


def _inproj(x, mod5, layer, g1, w_in_bf, tabs, gains, geo):
    n = x.shape[0]
    tm = PROJ_TM

    def mod_map(chunk):
        def f(i, j):
            b, _ = geo.locate(i * tm)
            return (layer, chunk, b, 0, 0)
        return f

    def tab_map(i, j):
        _, pos = geo.locate(i * tm)
        return (pos // tm, 0)

    mod_spec = lambda chunk: pl.BlockSpec((None, None, None, 1, D_MODEL), mod_map(chunk))
    tab_spec = pl.BlockSpec((tm, HEAD_DIM), tab_map)
    return pl.pallas_call(
        _inproj_kernel,
        out_shape=jax.ShapeDtypeStruct((n, PROJ_W), BF16),
        grid=(n // tm, PROJ_W // PROJ_TN),
        in_specs=[
            pl.BlockSpec((tm, D_MODEL), lambda i, j: (i, 0)),
            mod_spec(1), mod_spec(0),
            pl.BlockSpec((1, D_MODEL), lambda i, j: (0, 0)),
            pl.BlockSpec((D_MODEL, PROJ_TN), lambda i, j: (0, j)),
            tab_spec, tab_spec, tab_spec, tab_spec,
            pl.BlockSpec((4, HEAD_DIM), lambda i, j: (0, 0)),
        ],
        out_specs=pl.BlockSpec((tm, PROJ_TN), lambda i, j: (i, j)),
        scratch_shapes=[pltpu.VMEM((tm, D_MODEL), BF16), pltpu.VMEM((tm, PROJ_TN), F32)],
        compiler_params=_cparams(("parallel", "arbitrary")),
        name="inproj",
    )(x, mod5, mod5, g1, w_in_bf, tabs[0], tabs[1], tabs[2], tabs[3], gains)


def _ret_kernel(lg_ref, q_ref, k_ref, v_ref, *rest, reverse, geo):
    if reverse:
        o_ref, s_scr, tab_scr = rest
    else:
        sg_ref, ob_ref, gn_ref, o_ref, s_scr, tab_scr = rest
    i = pl.program_id(0)
    nchunks = pl.num_programs(0)
    c = RET_CHUNK
    heads = RET_W // HEAD_DIM

    @pl.when(i == 0)
    def _():
        row = lax.broadcasted_iota(I32, (c, c), 0).astype(F32)
        col = lax.broadcasted_iota(I32, (c, c), 1).astype(F32)
        for hh in range(heads):
            lg = lg_ref[hh]
            if reverse:
                diff = col - row
                decay = jnp.where(diff > 0, jnp.exp(lg * jnp.maximum(diff, 0.0)), 0.0)
                qdec = jnp.exp(lg * (c - row))
                kdec = jnp.exp(lg * row)
            else:
                diff = row - col
                decay = jnp.where(diff >= 0, jnp.exp(lg * jnp.maximum(diff, 0.0)), 0.0)
                qdec = jnp.exp(lg * (row + 1.0))
                kdec = jnp.exp(lg * (c - 1.0 - row))
            tab_scr[hh, 0] = decay
            tab_scr[hh, 1] = qdec
            tab_scr[hh, 2] = kdec
            tab_scr[hh, 3] = jnp.exp(jnp.zeros((c, c), F32) + lg * c)

    chunk = (nchunks - 1 - i) if reverse else i
    row0 = chunk * c
    _, pos = geo.locate(row0)
    boundary = (pos + c == geo.seq_len(row0)) if reverse else (pos == 0)

    @pl.when(boundary)
    def _():
        s_scr[...] = jnp.zeros_like(s_scr)

    for hh in range(heads):
        sl = slice(hh * HEAD_DIM, (hh + 1) * HEAD_DIM)
        qh = q_ref[:, sl]
        kh = k_ref[:, sl]
        vh = v_ref[:, sl]
        s = _dot_nt(qh, kh) * tab_scr[hh, 0]
        o = _dot(s.astype(BF16), vh)
        qd = (qh.astype(F32) * tab_scr[hh, 1]).astype(BF16)
        o = o + _dot(qd, s_scr[hh].astype(BF16))
        kd_t = (kh.astype(F32) * tab_scr[hh, 2]).T.astype(BF16)
        s_scr[hh] = s_scr[hh] * tab_scr[hh, 3] + _dot(kd_t, vh)
        if reverse:
            o_ref[:, sl] = o
        else:
            tot = o + ob_ref[:, sl]
            y = _rms(tot, gn_ref[:, sl])
            o_ref[:, sl] = (sg_ref[:, sl].astype(F32) * y).astype(BF16)


def _retention(proj, lg_f, lg_b, ret_gn, geo):
    n = proj.shape[0]
    nchunks = n // RET_CHUNK
    c = RET_CHUNK
    heads = RET_W // HEAD_DIM
    scratch = [pltpu.VMEM((heads, HEAD_DIM, HEAD_DIM), F32), pltpu.VMEM((heads, 4, c, c), F32)]
    smem = pl.BlockSpec(memory_space=pltpu.SMEM)

    def col_spec(colblk, rev):
        if rev:
            return pl.BlockSpec((c, RET_W), lambda i: (nchunks - 1 - i, colblk))
        return pl.BlockSpec((c, RET_W), lambda i: (i, colblk))

    o_bwd = pl.pallas_call(
        functools.partial(_ret_kernel, reverse=True, geo=geo),
        out_shape=jax.ShapeDtypeStruct((n, RET_W), F32),
        grid=(nchunks,),
        in_specs=[smem, col_spec(0, True), col_spec(1, True), col_spec(2, True)],
        out_specs=pl.BlockSpec((c, RET_W), lambda i: (nchunks - 1 - i, 0)),
        scratch_shapes=scratch,
        compiler_params=_cparams(("arbitrary",)),
        name="ret_bwd",
    )(lg_b, proj, proj, proj)
    return pl.pallas_call(
        functools.partial(_ret_kernel, reverse=False, geo=geo),
        out_shape=jax.ShapeDtypeStruct((n, RET_W), BF16),
        grid=(nchunks,),
        in_specs=[smem, col_spec(0, False), col_spec(1, False), col_spec(2, False), col_spec(3, False),
                  pl.BlockSpec((c, RET_W), lambda i: (i, 0)),
                  pl.BlockSpec((1, RET_W), lambda i: (0, 0))],
        out_specs=pl.BlockSpec((c, RET_W), lambda i: (i, 0)),
        scratch_shapes=scratch,
        compiler_params=_cparams(("arbitrary",)),
        name="ret_fwd",
    )(lg_f, proj, proj, proj, proj, o_bwd, ret_gn)


def _na_geometry(i, geo):
    rp, rs = geo.tp // GRID_W, geo.ts // GRID_W
    r0g = i * NA_QROWS
    in_p = r0g < geo.bp * rp
    rsmp = jnp.maximum(r0g - geo.bp * rp, 0)
    rows = jnp.where(in_p, rp, rs)
    lr0 = jnp.where(in_p, r0g % rp, rsmp % rs)
    slab = jnp.clip(lr0 - NA_WIN_R // 2, 0, rows - NA_SLAB_ROWS)
    return rows, lr0, slab, r0g - lr0


def _na_kernel(q_ref, k0, k1, k2, k3, v0, v1, v2, v3, tab_ref, o_ref):
    k = jnp.concatenate([k0[...], k1[...], k2[...], k3[...]], axis=0)
    v = jnp.concatenate([v0[...], v1[...], v2[...], v3[...]], axis=0)
    s = _chunked(lambda x: _dot_nt(x, k), q_ref[...], NA_DOT_ROWS) * QK_SCALE + tab_ref[...]
    e = jnp.exp(s - jnp.max(s, axis=-1, keepdims=True))
    prob = (e / jnp.sum(e, axis=-1, keepdims=True)).astype(BF16)
    o_ref[...] = _chunked(lambda x: _dot(x, v), prob, NA_DOT_ROWS).astype(BF16)


def _na_bias_table(rpb):
    rpb = rpb.astype(F32)
    rows = jnp.stack([rpb[:, NA_WIN_R - 1 - v:2 * NA_WIN_R - 1 - v, :] for v in range(NA_WIN_R)], axis=1)
    width = 2 * GRID_W
    lead = GRID_W - NA_WIN_C
    p = jnp.pad(rows, ((0, 0), (0, 0), (0, 0), (lead, width - lead - (2 * NA_WIN_C - 1))))
    skew = jnp.tile(p, GRID_W)[..., :GRID_W * (width - 1)].reshape(p.shape[:-1] + (GRID_W, width - 1))
    toe = skew[..., GRID_W - 1:]
    c = np.arange(GRID_W)[:, None]
    j = np.arange(GRID_W)[None, :]
    cs = np.clip(c - NA_WIN_C // 2, 0, GRID_W - NA_WIN_C)
    valid = (j >= cs) & (j < cs + NA_WIN_C)
    tab8 = jnp.where(valid[None, None, :, None, :], toe.transpose(0, 1, 3, 2, 4), MASK_VALUE)
    blocks = []
    for d in range(NA_SLAB_ROWS // NA_WIN_R + 1):
        per_row = []
        for p in range(NA_QROWS):
            first = (max(p - NA_WIN_R // 2, 0), p, min(p + NA_WIN_R // 2, NA_WIN_R))[d]
            win = tab8[:, d * (NA_WIN_R // 2) + p - first]
            per_row.append(jnp.pad(win, ((0, 0), (0, 0), (first, NA_SLAB_ROWS - NA_WIN_R - first), (0, 0)),
                                   constant_values=MASK_VALUE))
        blocks.append(jnp.stack(per_row, axis=1))
    return jnp.stack(blocks, axis=1).reshape(rpb.shape[0], len(blocks), NA_QROWS * GRID_W,
                                             NA_SLAB_ROWS * GRID_W)


def _neighbourhood(proj, bias_tab, geo):
    n = proj.shape[0]
    tq = NA_QROWS * GRID_W
    blk = NA_KBLK_ROWS * GRID_W
    nblk = NA_SLAB_ROWS // NA_KBLK_ROWS
    heads = NA_W // HEAD_DIM
    qcol, kcol, vcol = 4 * heads, 5 * heads, 6 * heads

    def kv_spec(col0, m):
        def f(h, i):
            _, _, slab, seq_row0 = _na_geometry(i, geo)
            return ((seq_row0 + slab) // NA_KBLK_ROWS + m, col0 + h)
        return pl.BlockSpec((blk, HEAD_DIM), f)

    def tab_map(h, i):
        _, lr0, slab, _ = _na_geometry(i, geo)
        return (h, (lr0 - slab) // (NA_WIN_R // 2), 0, 0)

    return pl.pallas_call(
        _na_kernel,
        out_shape=jax.ShapeDtypeStruct((n, NA_W), BF16),
        grid=(heads, n // tq),
        in_specs=[pl.BlockSpec((tq, HEAD_DIM), lambda h, i: (i, qcol + h))]
        + [kv_spec(kcol, m) for m in range(nblk)]
        + [kv_spec(vcol, m) for m in range(nblk)]
        + [pl.BlockSpec((None, None, tq, NA_SLAB_ROWS * GRID_W), tab_map)],
        out_specs=pl.BlockSpec((tq, HEAD_DIM), lambda h, i: (i, h)),
        compiler_params=_cparams(("parallel", "parallel")),
        name="natten",
    )(*([proj] * (1 + 2 * nblk)), bias_tab)


def _chunked(fn, x, rows):
    return jnp.concatenate([fn(x[c * rows:(c + 1) * rows]) for c in range(x.shape[0] // rows)], axis=0)


def _flash_kernel(q_ref, k_ref, v_ref, o_ref, m_scr, acc_scr, va_scr):
    ki = pl.program_id(3)
    groups = q_ref.shape[1] // HEAD_DIM
    tk = k_ref.shape[0]

    @pl.when(ki == 0)
    def _():
        m_scr[...] = jnp.full_like(m_scr, NEG_INF)
        acc_scr[...] = jnp.zeros_like(acc_scr)
        lane = lax.broadcasted_iota(I32, (tk, HEAD_DIM), 1)
        va_scr[:, HEAD_DIM:] = jnp.where(lane == 0, 1.0, 0.0).astype(BF16)

    va_scr[:, :HEAD_DIM] = v_ref[...]
    k = k_ref[...]
    va = va_scr[...]
    for gq in range(groups):
        q = q_ref[:, gq * HEAD_DIM:(gq + 1) * HEAD_DIM]
        s = _chunked(lambda x: _dot_nt(x, k), q, FLASH_DOT_ROWS)
        m_prev = m_scr[gq][:, :1]
        m_next = jnp.maximum(m_prev, jnp.max(s, axis=1, keepdims=True))
        alpha = jnp.exp2(m_prev - m_next)
        p = jnp.exp2(s - m_next).astype(BF16)
        acc_scr[gq] = alpha * acc_scr[gq] + _chunked(lambda x: _dot(x, va), p, FLASH_DOT_ROWS)
        m_scr[gq] = jnp.broadcast_to(m_next, m_scr.shape[1:])

    @pl.when(ki == pl.num_programs(3) - 1)
    def _():
        for gq in range(groups):
            acc = acc_scr[gq]
            o_ref[:, gq * HEAD_DIM:(gq + 1) * HEAD_DIM] = (
                acc[:, :HEAD_DIM] / acc[:, HEAD_DIM:HEAD_DIM + 1]).astype(BF16)


def _flash_group(proj, row0, batch, t):
    tq, tk = min(FLASH_TQ, t), min(FLASH_TK, t)
    assert row0 % tq == 0 and row0 % tk == 0 and t % tq == 0 and t % tk == 0
    kv_heads = GA_KV_W // HEAD_DIM
    qw = GA_W // kv_heads
    groups = qw // HEAD_DIM
    qcol0 = (4 * RET_W + 3 * NA_W) // qw
    kcol0 = (4 * RET_W + 3 * NA_W + GA_W) // HEAD_DIM
    vcol0 = kcol0 + kv_heads
    return pl.pallas_call(
        _flash_kernel,
        out_shape=jax.ShapeDtypeStruct((batch * t, GA_W), BF16),
        grid=(batch, kv_heads, t // tq, t // tk),
        in_specs=[
            pl.BlockSpec((tq, qw), lambda b, h, qi, ki: ((row0 + b * t) // tq + qi, qcol0 + h)),
            pl.BlockSpec((tk, HEAD_DIM), lambda b, h, qi, ki: ((row0 + b * t) // tk + ki, kcol0 + h)),
            pl.BlockSpec((tk, HEAD_DIM), lambda b, h, qi, ki: ((row0 + b * t) // tk + ki, vcol0 + h)),
        ],
        out_specs=pl.BlockSpec((tq, qw), lambda b, h, qi, ki: ((b * t) // tq + qi, h)),
        scratch_shapes=[pltpu.VMEM((groups, tq, HEAD_DIM), F32),
                        pltpu.VMEM((groups, tq, 2 * HEAD_DIM), F32),
                        pltpu.VMEM((tk, 2 * HEAD_DIM), BF16)],
        compiler_params=_cparams(("parallel", "parallel", "parallel", "arbitrary")),
        name="flash_gqa",
    )(proj, proj, proj)


def _outproj_kernel(yr_ref, yn_ref, yg_ref, x_ref, w_ref, non_ref, gon_ref, gt_ref, sc_ref, sh_ref, g2_ref,
                    x1_ref, h2_ref):
    ynn = _rms(yn_ref[...].astype(F32), non_ref[...]).astype(BF16)
    ygn = _rms(yg_ref[...].astype(F32), gon_ref[...]).astype(BF16)
    acc = _dot(yr_ref[...], w_ref[0:RET_W, :])
    acc = acc + _dot(ynn, w_ref[RET_W:RET_W + NA_W, :])
    acc = acc + _dot(ygn, w_ref[RET_W + NA_W:, :])
    x1 = x_ref[...] + gt_ref[...] * acc
    x1_ref[...] = x1
    h2_ref[...] = _rms(x1, g2_ref[...]) * (1.0 + sc_ref[...]) + sh_ref[...]


def _outproj(y_ret, y_na, y_ga, x, w_out_bf, na_on, ga_on, mod5, layer, g2, geo):
    n = x.shape[0]
    tm = OUT_TM

    def mod_spec(chunk):
        def f(i):
            b, _ = geo.locate(i * tm)
            return (layer, chunk, b, 0, 0)
        return pl.BlockSpec((None, None, None, 1, D_MODEL), f)

    row = lambda w: pl.BlockSpec((tm, w), lambda i: (i, 0))
    return pl.pallas_call(
        _outproj_kernel,
        out_shape=(jax.ShapeDtypeStruct((n, D_MODEL), F32), jax.ShapeDtypeStruct((n, D_MODEL), F32)),
        grid=(n // tm,),
        in_specs=[row(RET_W), row(NA_W), row(GA_W), row(D_MODEL),
                  pl.BlockSpec((D_MODEL, D_MODEL), lambda i: (0, 0)),
                  pl.BlockSpec((1, NA_W), lambda i: (0, 0)),
                  pl.BlockSpec((1, GA_W), lambda i: (0, 0)),
                  mod_spec(2), mod_spec(4), mod_spec(3),
                  pl.BlockSpec((1, D_MODEL), lambda i: (0, 0))],
        out_specs=(row(D_MODEL), row(D_MODEL)),
        compiler_params=_cparams(("parallel",)),
        name="outproj",
    )(y_ret, y_na, y_ga, x, w_out_bf, na_on, ga_on, mod5, mod5, mod5, g2)


def _first_index_of_max(vals, ids, axes, sentinel):
    m = vals
    for ax in axes:
        m = jnp.max(m, axis=ax, keepdims=True)
    cand = jnp.where(vals == m, ids, sentinel)
    for ax in axes:
        cand = jnp.min(cand, axis=ax, keepdims=True)
    return m, cand


def _router_kernel(h_ref, w_ref, b_ref, idx_ref, wgt_ref, rank_ref, cnt_ref, cnt_scr, tri_scr):
    i = pl.program_id(0)
    tm = h_ref.shape[0]

    @pl.when(i == 0)
    def _():
        cnt_scr[...] = jnp.zeros_like(cnt_scr)
        r = lax.broadcasted_iota(I32, (tm, tm), 0)
        c = lax.broadcasted_iota(I32, (tm, tm), 1)
        tri_scr[...] = jnp.where(r < c, 1.0, 0.0).astype(BF16)

    logits = lax.dot_general(w_ref[...], h_ref[...], (((1,), (1,)), ((), ())),
                             preferred_element_type=F32, precision=lax.Precision.HIGHEST)
    scores = _sigmoid(logits)
    choice = scores + b_ref[...][:, :1]
    shape3 = (N_GROUPS, GROUP_SIZE, tm)
    choice3 = choice.reshape(shape3)
    scores3 = scores.reshape(shape3)
    sub = lax.broadcasted_iota(I32, shape3, 1)
    grp = lax.broadcasted_iota(I32, shape3, 0)
    eid = grp * GROUP_SIZE + sub

    m1, i1 = _first_index_of_max(choice3, sub, (1,), GROUP_SIZE)
    rest = jnp.where(sub == i1, NEG_INF, choice3)
    m2 = jnp.max(rest, axis=1, keepdims=True)
    gscore = m1 + m2

    gid = lax.broadcasted_iota(I32, gscore.shape, 0)
    gsel = jnp.zeros(gscore.shape, F32)
    for _ in range(TOPK_GROUPS):
        _, gi = _first_index_of_max(gscore, gid, (0,), N_GROUPS)
        hit = gid == gi
        gsel = jnp.where(hit, 1.0, gsel)
        gscore = jnp.where(hit, NEG_INF, gscore)

    masked = jnp.where(gsel > 0.0, choice3, NEG_INF)
    onehots, ids, ws = [], [], []
    for _ in range(TOP_K):
        _, ei = _first_index_of_max(masked, eid, (1, 0), N_EXPERTS)
        hit = eid == ei
        onehots.append(hit)
        ids.append(ei.reshape(1, tm))
        ws.append(jnp.sum(jnp.where(hit, scores3, 0.0), axis=(0, 1), keepdims=True).reshape(1, tm))
        masked = jnp.where(hit, NEG_INF, masked)
    wsum = ws[0]
    for k in range(1, TOP_K):
        wsum = wsum + ws[k]

    sel = jnp.zeros(shape3, F32)
    for hit in onehots:
        sel = jnp.where(hit, 1.0, sel)
    sel2 = sel.reshape(N_EXPERTS, tm)
    before = _dot(sel2.astype(BF16), tri_scr[...]) + cnt_scr[:, :1]
    before3 = before.reshape(shape3)
    ranks = [jnp.sum(jnp.where(hit, before3, 0.0), axis=(0, 1), keepdims=True).reshape(1, tm)
             for hit in onehots]
    cnt_scr[...] = cnt_scr[...] + jnp.sum(sel2, axis=1, keepdims=True)

    idx_ref[...] = jnp.concatenate(ids, axis=0)
    wgt_ref[...] = jnp.concatenate([w / wsum * ROUTED_SCALE for w in ws], axis=0)
    rank_ref[...] = jnp.concatenate(ranks, axis=0).astype(I32)
    cnt_ref[...] = cnt_scr[...]


def _router(h2, w_router_t, bias_col):
    n = h2.shape[0]
    tm = ROUTER_TM
    out_blk = pl.BlockSpec((TOP_K, tm), lambda i: (0, i))
    return pl.pallas_call(
        _router_kernel,
        out_shape=(jax.ShapeDtypeStruct((TOP_K, n), I32), jax.ShapeDtypeStruct((TOP_K, n), F32),
                   jax.ShapeDtypeStruct((TOP_K, n), I32), jax.ShapeDtypeStruct((N_EXPERTS, HEAD_DIM), F32)),
        grid=(n // tm,),
        in_specs=[pl.BlockSpec((tm, D_MODEL), lambda i: (i, 0)),
                  pl.BlockSpec((N_EXPERTS, D_MODEL), lambda i: (0, 0)),
                  pl.BlockSpec((N_EXPERTS, HEAD_DIM), lambda i: (0, 0))],
        out_specs=(out_blk, out_blk, out_blk, pl.BlockSpec((N_EXPERTS, HEAD_DIM), lambda i: (0, 0))),
        scratch_shapes=[pltpu.VMEM((N_EXPERTS, HEAD_DIM), F32), pltpu.VMEM((tm, tm), BF16)],
        compiler_params=_cparams(("arbitrary",)),
        name="router",
    )(h2, w_router_t, bias_col)


def _row_copy(src, src_row, dst, dst_row, sem, rows=1):
    return pltpu.make_async_copy(src.at[pl.ds(src_row, rows)], dst.at[pl.ds(dst_row, rows)], sem)


def _zero_fill_padding(fill_ref, end_ref, xs_hbm, zero_scr, zsem, wait):
    def go(copy):
        copy.wait() if wait else copy.start()

    def per_expert(e, carry):
        def per_row(r, c):
            go(pltpu.make_async_copy(zero_scr.at[pl.ds(0, 1)], xs_hbm.at[pl.ds(r, 1)], zsem))
            return c
        return lax.fori_loop(fill_ref[e], end_ref[e], per_row, carry)

    lax.fori_loop(0, N_EXPERTS, per_expert, 0)
    total = end_ref[N_EXPERTS - 1]
    n_tiles = xs_hbm.shape[0] // FFN_TM
    for tile in range(n_tiles - N_EXPERTS, n_tiles):
        @pl.when(tile * FFN_TM >= total)
        def _(tile=tile):
            go(pltpu.make_async_copy(zero_scr, xs_hbm.at[pl.ds(tile * FFN_TM, FFN_TM)], zsem))


def _dispatch_kernel(fill_ref, end_ref, pos_ref, h_ref, xs_hbm, zero_scr, sem, zsem):
    i = pl.program_id(0)
    tm = pos_ref.shape[1]

    @pl.when(i == 0)
    def _():
        zero_scr[...] = jnp.zeros_like(zero_scr)
        _zero_fill_padding(fill_ref, end_ref, xs_hbm, zero_scr, zsem, wait=False)
        _zero_fill_padding(fill_ref, end_ref, xs_hbm, zero_scr, zsem, wait=True)

    def issue(t, carry):
        for k in range(TOP_K):
            _row_copy(h_ref, t, xs_hbm, pos_ref[k, t], sem).start()
        return carry

    lax.fori_loop(0, tm, issue, 0)
    def drain(t, carry):
        _row_copy(h_ref, 0, xs_hbm, 0, sem, rows=TOP_K).wait()
        return carry

    lax.fori_loop(0, tm, drain, 0)


def _dispatch(h2, pos, seg_fill, seg_end, n_slots):
    n = h2.shape[0]
    tm = DISPATCH_TM
    smem_blk = pl.BlockSpec((TOP_K, tm), lambda i, *_: (0, i), memory_space=pltpu.SMEM)
    return pl.pallas_call(
        _dispatch_kernel,
        out_shape=jax.ShapeDtypeStruct((n_slots, D_MODEL), F32),
        grid_spec=pltpu.PrefetchScalarGridSpec(
            num_scalar_prefetch=2,
            grid=(n // tm,),
            in_specs=[smem_blk, pl.BlockSpec((tm, D_MODEL), lambda i, *_: (i, 0))],
            out_specs=pl.BlockSpec(memory_space=pl.ANY),
            scratch_shapes=[pltpu.VMEM((FFN_TM, D_MODEL), F32),
                            pltpu.SemaphoreType.DMA(()), pltpu.SemaphoreType.DMA(())],
        ),
        compiler_params=_cparams(("arbitrary",)),
        name="moe_dispatch",
    )(seg_fill, seg_end, pos, h2)


def _ffn_kernel(te_ref, nv_ref, x_ref, wg_ref, wu_ref, wd_ref, o_ref, wg_bf, wu_bf, wd_bf):
    i = pl.program_id(0)
    nvalid = nv_ref[i]

    @pl.when((i == 0) | (te_ref[i] != te_ref[jnp.maximum(i - 1, 0)]))
    def _():
        wg_bf[...] = wg_ref[...].astype(BF16)
        wu_bf[...] = wu_ref[...].astype(BF16)
        wd_bf[...] = wd_ref[...].astype(BF16)

    @pl.when(nvalid > 0)
    def _():
        x = x_ref[...].astype(BF16)
        hg = _dot(x, wg_bf[...])
        hu = _dot(x, wu_bf[...])
        h = (hg * _sigmoid(hg) * hu).astype(BF16)
        o_ref[...] = _dot(h, wd_bf[...])

    @pl.when(nvalid == 0)
    def _():
        o_ref[...] = jnp.zeros_like(o_ref)


def _expert_ffn(x_sorted, tile_expert, tile_valid, wg, wu, wd, layer):
    n_slots = x_sorted.shape[0]
    tm = FFN_TM
    return pl.pallas_call(
        _ffn_kernel,
        out_shape=jax.ShapeDtypeStruct((n_slots, D_MODEL), F32),
        grid_spec=pltpu.PrefetchScalarGridSpec(
            num_scalar_prefetch=2,
            grid=(n_slots // tm,),
            in_specs=[pl.BlockSpec((tm, D_MODEL), lambda i, te, nv: (i, 0)),
                      pl.BlockSpec((None, None, D_MODEL, EXPERT_FF), lambda i, te, nv: (layer, te[i], 0, 0)),
                      pl.BlockSpec((None, None, D_MODEL, EXPERT_FF), lambda i, te, nv: (layer, te[i], 0, 0)),
                      pl.BlockSpec((None, None, EXPERT_FF, D_MODEL), lambda i, te, nv: (layer, te[i], 0, 0))],
            out_specs=pl.BlockSpec((tm, D_MODEL), lambda i, te, nv: (i, 0)),
            scratch_shapes=[pltpu.VMEM((D_MODEL, EXPERT_FF), BF16), pltpu.VMEM((D_MODEL, EXPERT_FF), BF16),
                            pltpu.VMEM((EXPERT_FF, D_MODEL), BF16)],
        ),
        compiler_params=_cparams(("arbitrary",)),
        name="moe_ffn",
    )(tile_expert, tile_valid, x_sorted, wg, wu, wd)


def _combine_kernel(pos_ref, posn_ref, wgt_ref, x1_ref, h_ref, gt_ref,
                    wsg_ref, wsu_ref, wsd_ref, y_hbm, o_ref, ybuf, sems):
    i = pl.program_id(0)
    nsteps = pl.num_programs(0)
    tm = x1_ref.shape[0]

    def issue(slot, pr):
        def body(t, carry):
            for k in range(TOP_K):
                pltpu.make_async_copy(y_hbm.at[pl.ds(pr[k, t], 1)], ybuf.at[slot, k, pl.ds(t, 1)],
                                      sems.at[slot]).start()
            return carry
        lax.fori_loop(0, tm, body, 0)

    slot = i % 2

    @pl.when(i == 0)
    def _():
        issue(0, pos_ref)

    @pl.when(i + 1 < nsteps)
    def _():
        issue(1 - slot, posn_ref)

    hb = h_ref[...].astype(BF16)
    hg = _dot(hb, wsg_ref[...])
    hu = _dot(hb, wsu_ref[...])
    shared = _dot((hg * _sigmoid(hg) * hu).astype(BF16), wsd_ref[...])

    for k in range(TOP_K):
        pltpu.make_async_copy(y_hbm.at[pl.ds(0, tm)], ybuf.at[slot, k], sems.at[slot]).wait()

    acc = shared
    for k in range(TOP_K):
        acc = acc + wgt_ref[:, k:k + 1] * ybuf[slot, k]
    o_ref[...] = x1_ref[...] + gt_ref[...] * acc


def _combine(y_sorted, pos, wgt_t, x1, h2, mod5, layer, wsg, wsu, wsd, geo):
    n = x1.shape[0]
    tm = COMBINE_TM
    nsteps = n // tm
    smem_cur = pl.BlockSpec((TOP_K, tm), lambda i: (0, i), memory_space=pltpu.SMEM)
    smem_next = pl.BlockSpec((TOP_K, tm), lambda i: (0, jnp.minimum(i + 1, nsteps - 1)),
                             memory_space=pltpu.SMEM)

    def gt_map(i):
        b, _ = geo.locate(i * tm)
        return (layer, 5, b, 0, 0)

    row = pl.BlockSpec((tm, D_MODEL), lambda i: (i, 0))
    whole = lambda a, b: pl.BlockSpec((a, b), lambda i: (0, 0))
    return pl.pallas_call(
        _combine_kernel,
        out_shape=jax.ShapeDtypeStruct((n, D_MODEL), F32),
        grid=(nsteps,),
        in_specs=[smem_cur, smem_next,
                  pl.BlockSpec((tm, TOP_K), lambda i: (i, 0)),
                  row, row,
                  pl.BlockSpec((None, None, None, 1, D_MODEL), gt_map),
                  whole(D_MODEL, EXPERT_FF), whole(D_MODEL, EXPERT_FF), whole(EXPERT_FF, D_MODEL),
                  pl.BlockSpec(memory_space=pl.ANY)],
        out_specs=row,
        scratch_shapes=[pltpu.VMEM((2, TOP_K, tm, D_MODEL), F32), pltpu.SemaphoreType.DMA((2,))],
        compiler_params=_cparams(("arbitrary",)),
        name="moe_combine",
    )(pos, pos, wgt_t, x1, h2, mod5, wsg, wsu, wsd, y_sorted)


def _rope_tables(t_max):
    pos = jnp.arange(t_max, dtype=jnp.int32)

    def angles(p, n_pairs):
        inv = ROPE_BASE ** (-jnp.arange(n_pairs, dtype=F32) / n_pairs)
        ang = p.astype(F32)[:, None] * inv[None, :]
        return jnp.cos(ang), jnp.sin(ang)

    c, s = angles(pos, HEAD_DIM // 2)
    cr, sr = angles(pos // GRID_W, HEAD_DIM // 4)
    cc, sc = angles(pos % GRID_W, HEAD_DIM // 4)
    return (jnp.concatenate([c, c], -1), jnp.concatenate([-s, s], -1),
            jnp.concatenate([cr, cr, cc, cc], -1), jnp.concatenate([-sr, sr, -sc, sc], -1))


def _moe_plan(counts, n_slots):
    padded = (counts + FFN_TM - 1) // FFN_TM * FFN_TM
    seg_end = jnp.cumsum(padded)
    seg_start = seg_end - padded
    tile_row0 = jnp.arange(n_slots // FFN_TM, dtype=I32) * FFN_TM
    tile_expert = jnp.minimum(jnp.sum(tile_row0[:, None] >= seg_end[None, :], axis=1), N_EXPERTS - 1).astype(I32)
    tile_valid = jnp.clip(seg_start[tile_expert] + counts[tile_expert] - tile_row0, 0, FFN_TM).astype(I32)
    return (seg_start.astype(I32), (seg_start + counts).astype(I32), seg_end.astype(I32),
            tile_expert, tile_valid)


def _layer(x, mod5, layer, p, tabs, geo):
    n = x.shape[0]
    proj = _inproj(x, mod5, layer, p["g1"], p["w_in"], tabs, p["gains"], geo)
    y_ret = _retention(proj, p["lg_f"], p["lg_b"], p["ret_gn"], geo)
    y_na = _neighbourhood(proj, p["na_tab"], geo)
    y_ga = jnp.concatenate([_flash_group(proj, 0, geo.bp, geo.tp),
                            _flash_group(proj, geo.n_p, geo.bs, geo.ts)], axis=0)
    x1, h2 = _outproj(y_ret, y_na, y_ga, x, p["w_out"], p["na_on"], p["ga_on"], mod5, layer, p["g2"], geo)
    idx, wgt, rank, cnt = _router(h2, p["w_router_t"], p["router_bias"])
    n_slots = n * TOP_K + N_EXPERTS * FFN_TM
    seg_start, seg_fill, seg_end, tile_expert, tile_valid = _moe_plan(cnt[:, 0].astype(I32), n_slots)
    experts = jnp.arange(N_EXPERTS, dtype=I32)
    pos = rank + jnp.sum(jnp.where(idx[:, :, None] == experts, seg_start, 0), axis=-1)
    x_sorted = _dispatch(h2, pos, seg_fill, seg_end, n_slots)
    y_sorted = _expert_ffn(x_sorted, tile_expert, tile_valid, p["w_eg"], p["w_eu"], p["w_ed"], layer)
    return _combine(y_sorted, pos, wgt.T, x1, h2, mod5, layer, p["w_sg"], p["w_su"], p["w_sd"], geo)


def kernel(x_prompt, x_sample, c_prompt, c_sample, w_ada, b_ada, norm1, w_in, ret_decay_fwd, ret_decay_bwd, ret_norm, na_q_norm, na_k_norm, na_rpb, na_out_norm, ga_q_norm, ga_k_norm, ga_out_norm, w_out, norm2, w_router, router_bias, w_exp_gate, w_exp_up, w_exp_down, w_sh_gate, w_sh_up, w_sh_down):
    bp, tp, d = x_prompt.shape
    bs, ts, _ = x_sample.shape
    depth = w_ada.shape[0]
    geo = Geom(bp, tp, bs, ts)
    assert d == D_MODEL and w_in.shape[-1] == PROJ_W
    for t in (tp, ts):
        assert t % max(PROJ_TM, FLASH_TQ, NA_SLAB_ROWS * GRID_W) == 0 and t % min(FLASH_TK, t) == 0

    x = jnp.concatenate([x_prompt.reshape(bp * tp, d), x_sample.reshape(bs * ts, d)], axis=0)
    c_all = jnp.concatenate([c_prompt, c_sample], axis=0)
    c_pad = jnp.zeros((8 * pl.cdiv(geo.nb, 8), d), F32).at[:geo.nb].set(c_all)
    mod = _adaln(c_pad, w_ada, b_ada)
    rows = mod.shape[1]
    mod5 = mod.reshape(depth, rows, N_MOD, 1, d).transpose(0, 2, 1, 3, 4)
    tabs = _rope_tables(max(tp, ts))

    for l in range(depth):
        p = {
            "g1": norm1[l].reshape(1, d),
            "w_in": w_in[l].astype(BF16),
            "gains": jnp.stack([na_q_norm[l], na_k_norm[l], ga_q_norm[l], ga_k_norm[l]]),
            "lg_f": jax.nn.log_sigmoid(ret_decay_fwd[l].astype(F32)),
            "lg_b": jax.nn.log_sigmoid(ret_decay_bwd[l].astype(F32)),
            "ret_gn": ret_norm[l].reshape(1, RET_W),
            "na_tab": _na_bias_table(na_rpb[l]),
            "na_on": na_out_norm[l].reshape(1, NA_W),
            "ga_on": ga_out_norm[l].reshape(1, GA_W),
            "w_out": w_out[l].astype(BF16),
            "g2": norm2[l].reshape(1, d),
            "w_router_t": w_router[l].astype(F32).T,
            "router_bias": jnp.broadcast_to(router_bias[l].astype(F32)[:, None], (N_EXPERTS, HEAD_DIM)),
            "w_eg": w_exp_gate,
            "w_eu": w_exp_up,
            "w_ed": w_exp_down,
            "w_sg": w_sh_gate[l].astype(BF16),
            "w_su": w_sh_up[l].astype(BF16),
            "w_sd": w_sh_down[l].astype(BF16),
        }
        x = _layer(x, mod5, l, p, tabs, geo)

    y_prompt = x[:geo.n_p].reshape(bp, tp, d)
    y_sample = x[geo.n_p:].reshape(bs, ts, d)
    return (y_prompt, y_sample)
```

```python
import dataclasses
import functools

import numpy as np
import jax
import jax.numpy as jnp
from jax import lax
from jax.experimental import pallas as pl
from jax.experimental.pallas import tpu as pltpu

F32 = jnp.float32
BF16 = jnp.bfloat16
I32 = jnp.int32

D_MODEL = 2048
HEAD_DIM = 128
RET_W = 512
NA_W = 512
GA_W = 1024
GA_KV_W = 256
PROJ_W = 4 * RET_W + 3 * NA_W + GA_W + 2 * GA_KV_W
RET_CHUNK = 128
GRID_W = 64
NA_WIN_R = 8
NA_WIN_C = 16
ROPE_BASE = 10000.0
N_EXPERTS = 64
N_GROUPS = 8
GROUP_SIZE = N_EXPERTS // N_GROUPS
TOPK_GROUPS = 4
TOP_K = 8
EXPERT_FF = 512
ROUTED_SCALE = 2.5
N_MOD = 6
EPS = 1e-6
QK_SCALE = HEAD_DIM ** -0.5
LOG2E = 1.4426950408889634
NEG_INF = float("-inf")
MASK_VALUE = -1e30

VMEM_LIMIT_BYTES = 56 * 1024 * 1024

PROJ_TN = 512
PROJ_TM = 1024
NA_QROWS = 8
NA_SLAB_ROWS = 16
NA_KBLK_ROWS = 4
NA_DOT_ROWS = 256
FLASH_TQ = 1024
FLASH_TK = 4096
FLASH_DOT_ROWS = 256
OUT_TM = 256
ROUTER_TM = 512
DISPATCH_TM = 256
FFN_TM = 512
COMBINE_TM = 128
ADALN_TN = 1024


@dataclasses.dataclass(frozen=True)
class Geom:
    bp: int
    tp: int
    bs: int
    ts: int

    @property
    def n_p(self):
        return self.bp * self.tp

    @property
    def n(self):
        return self.bp * self.tp + self.bs * self.ts

    @property
    def nb(self):
        return self.bp + self.bs

    def locate(self, row):
        in_p = row < self.n_p
        rs = jnp.maximum(row - self.n_p, 0)
        b = jnp.where(in_p, row // self.tp, self.bp + rs // self.ts)
        pos = jnp.where(in_p, row % self.tp, rs % self.ts)
        return b, pos

    def seq_len(self, row):
        return jnp.where(row < self.n_p, self.tp, self.ts)


def _cparams(sem, vmem=VMEM_LIMIT_BYTES):
    return pltpu.CompilerParams(dimension_semantics=sem, vmem_limit_bytes=vmem)


def _dot(a, b):
    return jnp.dot(a, b, preferred_element_type=F32)


def _dot_nt(a, b):
    return lax.dot_general(a, b, (((1,), (1,)), ((), ())), preferred_element_type=F32)


def _sigmoid(x):
    return 1.0 / (1.0 + jnp.exp(-x))


def _rms(x, g):
    return x * lax.rsqrt(jnp.mean(x * x, axis=-1, keepdims=True) + EPS) * g


def _adaln_kernel(c_ref, w_ref, b_ref, o_ref):
    c = c_ref[...]
    s = c * _sigmoid(c)
    o_ref[...] = jnp.dot(s, w_ref[...], preferred_element_type=F32,
                         precision=lax.Precision.HIGHEST) + b_ref[...]


def _adaln(c_pad, w_ada, b_ada):
    depth, d, w = w_ada.shape
    rows = c_pad.shape[0]
    return pl.pallas_call(
        _adaln_kernel,
        out_shape=jax.ShapeDtypeStruct((depth, rows, w), F32),
        grid=(depth, w // ADALN_TN),
        in_specs=[
            pl.BlockSpec((rows, d), lambda l, j: (0, 0)),
            pl.BlockSpec((None, d, ADALN_TN), lambda l, j: (l, 0, j)),
            pl.BlockSpec((None, 1, ADALN_TN), lambda l, j: (l, 0, j)),
        ],
        out_specs=pl.BlockSpec((None, rows, ADALN_TN), lambda l, j: (l, 0, j)),
        compiler_params=_cparams(("parallel", "parallel")),
        name="adaln",
    )(c_pad, w_ada, b_ada.reshape(depth, 1, w))


def _rope_ret(x, cos, sin):
    return x * cos + pltpu.roll(x, HEAD_DIM // 2, 1) * sin


def _rope_axial(x, cos, sin):
    lane = lax.broadcasted_iota(I32, x.shape, 1)
    first = (lane & (HEAD_DIM // 4)) == 0
    rot = jnp.where(first, pltpu.roll(x, HEAD_DIM - HEAD_DIM // 4, 1), pltpu.roll(x, HEAD_DIM // 4, 1))
    return x * cos + rot * sin


def _inproj_kernel(x_ref, sc_ref, sh_ref, g1_ref, w_ref, rc_ref, rs_ref, ac_ref, as_ref, gains_ref,
                   o_ref, h_scr, acc_scr):
    j = pl.program_id(1)

    @pl.when(j == 0)
    def _():
        h = _rms(x_ref[...], g1_ref[...]) * (1.0 + sc_ref[...]) + sh_ref[...]
        h_scr[...] = h.astype(BF16)

    acc_scr[...] = _dot(h_scr[...], w_ref[...])
    heads = PROJ_TN // HEAD_DIM

    def head(hh):
        return acc_scr[:, hh * HEAD_DIM:(hh + 1) * HEAD_DIM]

    def put(hh, val):
        o_ref[:, hh * HEAD_DIM:(hh + 1) * HEAD_DIM] = val.astype(BF16)

    @pl.when(j == 0)
    def _():
        for hh in range(heads):
            put(hh, _rope_ret(head(hh), rc_ref[...], rs_ref[...]))

    @pl.when(j == 1)
    def _():
        for hh in range(heads):
            put(hh, _rope_ret(head(hh), rc_ref[...], rs_ref[...]) * QK_SCALE)

    @pl.when((j == 2) | (j == 6))
    def _():
        o_ref[...] = acc_scr[...].astype(BF16)

    @pl.when(j == 3)
    def _():
        a = acc_scr[...]
        o_ref[...] = (a * _sigmoid(a)).astype(BF16)

    @pl.when(j == 4)
    def _():
        for hh in range(heads):
            put(hh, _rms(head(hh), gains_ref[0:1, :]))

    @pl.when(j == 5)
    def _():
        for hh in range(heads):
            put(hh, _rms(head(hh), gains_ref[1:2, :]))

    @pl.when((j == 7) | (j == 8))
    def _():
        for hh in range(heads):
            put(hh, _rope_axial(_rms(head(hh), gains_ref[2:3, :]), ac_ref[...], as_ref[...])
                * (QK_SCALE * LOG2E))

    @pl.when(j == 9)
    def _():
        for hh in range(2):
            put(hh, _rope_axial(_rms(head(hh), gains_ref[3:4, :]), ac_ref[...], as_ref[...]))
        o_ref[:, 2 * HEAD_DIM:] = acc_scr[:, 2 * HEAD_DIM:].astype(BF16)


def _inproj(x, mod5, layer, g1, w_in_bf, tabs, gains, geo):
    n = x.shape[0]
    tm = PROJ_TM

    def mod_map(chunk):
        def f(i, j):
            b, _ = geo.locate(i * tm)
            return (layer, chunk, b, 0, 0)
        return f

    def tab_map(i, j):
        _, pos = geo.locate(i * tm)
        return (pos // tm, 0)

    mod_spec = lambda chunk: pl.BlockSpec((None, None, None, 1, D_MODEL), mod_map(chunk))
    tab_spec = pl.BlockSpec((tm, HEAD_DIM), tab_map)
    return pl.pallas_call(
        _inproj_kernel,
        out_shape=jax.ShapeDtypeStruct((n, PROJ_W), BF16),
        grid=(n // tm, PROJ_W // PROJ_TN),
        in_specs=[
            pl.BlockSpec((tm, D_MODEL), lambda i, j: (i, 0)),
            mod_spec(1), mod_spec(0),
            pl.BlockSpec((1, D_MODEL), lambda i, j: (0, 0)),
            pl.BlockSpec((D_MODEL, PROJ_TN), lambda i, j: (0, j)),
            tab_spec, tab_spec, tab_spec, tab_spec,
            pl.BlockSpec((4, HEAD_DIM), lambda i, j: (0, 0)),
        ],
        out_specs=pl.BlockSpec((tm, PROJ_TN), lambda i, j: (i, j)),
        scratch_shapes=[pltpu.VMEM((tm, D_MODEL), BF16), pltpu.VMEM((tm, PROJ_TN), F32)],
        compiler_params=_cparams(("parallel", "arbitrary")),
        name="inproj",
    )(x, mod5, mod5, g1, w_in_bf, tabs[0], tabs[1], tabs[2], tabs[3], gains)


def _ret_kernel(lg_ref, q_ref, k_ref, v_ref, *rest, reverse, geo):
    if reverse:
        o_ref, s_scr, tab_scr = rest
    else:
        sg_ref, ob_ref, gn_ref, o_ref, s_scr, tab_scr = rest
    i = pl.program_id(0)
    nchunks = pl.num_programs(0)
    c = RET_CHUNK
    heads = RET_W // HEAD_DIM

    @pl.when(i == 0)
    def _():
        row = lax.broadcasted_iota(I32, (c, c), 0).astype(F32)
        col = lax.broadcasted_iota(I32, (c, c), 1).astype(F32)
        for hh in range(heads):
            lg = lg_ref[hh]
            if reverse:
                diff = col - row
                decay = jnp.where(diff > 0, jnp.exp(lg * jnp.maximum(diff, 0.0)), 0.0)
                qdec = jnp.exp(lg * (c - row))
                kdec = jnp.exp(lg * row)
            else:
                diff = row - col
                decay = jnp.where(diff >= 0, jnp.exp(lg * jnp.maximum(diff, 0.0)), 0.0)
                qdec = jnp.exp(lg * (row + 1.0))
                kdec = jnp.exp(lg * (c - 1.0 - row))
            tab_scr[hh, 0] = decay
            tab_scr[hh, 1] = qdec
            tab_scr[hh, 2] = kdec
            tab_scr[hh, 3] = jnp.exp(jnp.zeros((c, c), F32) + lg * c)

    chunk = (nchunks - 1 - i) if reverse else i
    row0 = chunk * c
    _, pos = geo.locate(row0)
    boundary = (pos + c == geo.seq_len(row0)) if reverse else (pos == 0)

    @pl.when(boundary)
    def _():
        s_scr[...] = jnp.zeros_like(s_scr)

    for hh in range(heads):
        sl = slice(hh * HEAD_DIM, (hh + 1) * HEAD_DIM)
        qh = q_ref[:, sl]
        kh = k_ref[:, sl]
        vh = v_ref[:, sl]
        s = _dot_nt(qh, kh) * tab_scr[hh, 0]
        o = _dot(s.astype(BF16), vh)
        qd = (qh.astype(F32) * tab_scr[hh, 1]).astype(BF16)
        o = o + _dot(qd, s_scr[hh].astype(BF16))
        kd_t = (kh.astype(F32) * tab_scr[hh, 2]).T.astype(BF16)
        s_scr[hh] = s_scr[hh] * tab_scr[hh, 3] + _dot(kd_t, vh)
        if reverse:
            o_ref[:, sl] = o
        else:
            tot = o + ob_ref[:, sl]
            y = _rms(tot, gn_ref[:, sl])
            o_ref[:, sl] = (sg_ref[:, sl].astype(F32) * y).astype(BF16)


def _retention(proj, lg_f, lg_b, ret_gn, geo):
    n = proj.shape[0]
    nchunks = n // RET_CHUNK
    c = RET_CHUNK
    heads = RET_W // HEAD_DIM
    scratch = [pltpu.VMEM((heads, HEAD_DIM, HEAD_DIM), F32), pltpu.VMEM((heads, 4, c, c), F32)]
    smem = pl.BlockSpec(memory_space=pltpu.SMEM)

    def col_spec(colblk, rev):
        if rev:
            return pl.BlockSpec((c, RET_W), lambda i: (nchunks - 1 - i, colblk))
        return pl.BlockSpec((c, RET_W), lambda i: (i, colblk))

    o_bwd = pl.pallas_call(
        functools.partial(_ret_kernel, reverse=True, geo=geo),
        out_shape=jax.ShapeDtypeStruct((n, RET_W), F32),
        grid=(nchunks,),
        in_specs=[smem, col_spec(0, True), col_spec(1, True), col_spec(2, True)],
        out_specs=pl.BlockSpec((c, RET_W), lambda i: (nchunks - 1 - i, 0)),
        scratch_shapes=scratch,
        compiler_params=_cparams(("arbitrary",)),
        name="ret_bwd",
    )(lg_b, proj, proj, proj)
    return pl.pallas_call(
        functools.partial(_ret_kernel, reverse=False, geo=geo),
        out_shape=jax.ShapeDtypeStruct((n, RET_W), BF16),
        grid=(nchunks,),
        in_specs=[smem, col_spec(0, False), col_spec(1, False), col_spec(2, False), col_spec(3, False),
                  pl.BlockSpec((c, RET_W), lambda i: (i, 0)),
                  pl.BlockSpec((1, RET_W), lambda i: (0, 0))],
        out_specs=pl.BlockSpec((c, RET_W), lambda i: (i, 0)),
        scratch_shapes=scratch,
        compiler_params=_cparams(("arbitrary",)),
        name="ret_fwd",
    )(lg_f, proj, proj, proj, proj, o_bwd, ret_gn)


def _na_geometry(i, geo):
    rp, rs = geo.tp // GRID_W, geo.ts // GRID_W
    r0g = i * NA_QROWS
    in_p = r0g < geo.bp * rp
    rsmp = jnp.maximum(r0g - geo.bp * rp, 0)
    rows = jnp.where(in_p, rp, rs)
    lr0 = jnp.where(in_p, r0g % rp, rsmp % rs)
    slab = jnp.clip(lr0 - NA_WIN_R // 2, 0, rows - NA_SLAB_ROWS)
    return rows, lr0, slab, r0g - lr0


def _na_kernel(q_ref, k0, k1, k2, k3, v0, v1, v2, v3, tab_ref, o_ref):
    k = jnp.concatenate([k0[...], k1[...], k2[...], k3[...]], axis=0)
    v = jnp.concatenate([v0[...], v1[...], v2[...], v3[...]], axis=0)
    s = _chunked(lambda x: _dot_nt(x, k), q_ref[...], NA_DOT_ROWS) * QK_SCALE + tab_ref[...]
    e = jnp.exp(s - jnp.max(s, axis=-1, keepdims=True))
    prob = (e / jnp.sum(e, axis=-1, keepdims=True)).astype(BF16)
    o_ref[...] = _chunked(lambda x: _dot(x, v), prob, NA_DOT_ROWS).astype(BF16)


def _na_bias_table(rpb):
    rpb = rpb.astype(F32)
    rows = jnp.stack([rpb[:, NA_WIN_R - 1 - v:2 * NA_WIN_R - 1 - v, :] for v in range(NA_WIN_R)], axis=1)
    width = 2 * GRID_W
    lead = GRID_W - NA_WIN_C
    p = jnp.pad(rows, ((0, 0), (0, 0), (0, 0), (lead, width - lead - (2 * NA_WIN_C - 1))))
    skew = jnp.tile(p, GRID_W)[..., :GRID_W * (width - 1)].reshape(p.shape[:-1] + (GRID_W, width - 1))
    toe = skew[..., GRID_W - 1:]
    c = np.arange(GRID_W)[:, None]
    j = np.arange(GRID_W)[None, :]
    cs = np.clip(c - NA_WIN_C // 2, 0, GRID_W - NA_WIN_C)
    valid = (j >= cs) & (j < cs + NA_WIN_C)
    tab8 = jnp.where(valid[None, None, :, None, :], toe.transpose(0, 1, 3, 2, 4), MASK_VALUE)
    blocks = []
    for d in range(NA_SLAB_ROWS // NA_WIN_R + 1):
        per_row = []
        for p in range(NA_QROWS):
            first = (max(p - NA_WIN_R // 2, 0), p, min(p + NA_WIN_R // 2, NA_WIN_R))[d]
            win = tab8[:, d * (NA_WIN_R // 2) + p - first]
            per_row.append(jnp.pad(win, ((0, 0), (0, 0), (first, NA_SLAB_ROWS - NA_WIN_R - first), (0, 0)),
                                   constant_values=MASK_VALUE))
        blocks.append(jnp.stack(per_row, axis=1))
    return jnp.stack(blocks, axis=1).reshape(rpb.shape[0], len(blocks), NA_QROWS * GRID_W,
                                             NA_SLAB_ROWS * GRID_W)


def _neighbourhood(proj, bias_tab, geo):
    n = proj.shape[0]
    tq = NA_QROWS * GRID_W
    blk = NA_KBLK_ROWS * GRID_W
    nblk = NA_SLAB_ROWS // NA_KBLK_ROWS
    heads = NA_W // HEAD_DIM
    qcol, kcol, vcol = 4 * heads, 5 * heads, 6 * heads

    def kv_spec(col0, m):
        def f(h, i):
            _, _, slab, seq_row0 = _na_geometry(i, geo)
            return ((seq_row0 + slab) // NA_KBLK_ROWS + m, col0 + h)
        return pl.BlockSpec((blk, HEAD_DIM), f)

    def tab_map(h, i):
        _, lr0, slab, _ = _na_geometry(i, geo)
        return (h, (lr0 - slab) // (NA_WIN_R // 2), 0, 0)

    return pl.pallas_call(
        _na_kernel,
        out_shape=jax.ShapeDtypeStruct((n, NA_W), BF16),
        grid=(heads, n // tq),
        in_specs=[pl.BlockSpec((tq, HEAD_DIM), lambda h, i: (i, qcol + h))]
        + [kv_spec(kcol, m) for m in range(nblk)]
        + [kv_spec(vcol, m) for m in range(nblk)]
        + [pl.BlockSpec((None, None, tq, NA_SLAB_ROWS * GRID_W), tab_map)],
        out_specs=pl.BlockSpec((tq, HEAD_DIM), lambda h, i: (i, h)),
        compiler_params=_cparams(("parallel", "parallel")),
        name="natten",
    )(*([proj] * (1 + 2 * nblk)), bias_tab)


def _chunked(fn, x, rows):
    return jnp.concatenate([fn(x[c * rows:(c + 1) * rows]) for c in range(x.shape[0] // rows)], axis=0)


def _flash_kernel(q_ref, k_ref, v_ref, o_ref, m_scr, acc_scr, va_scr):
    ki = pl.program_id(3)
    groups = q_ref.shape[1] // HEAD_DIM
    tk = k_ref.shape[0]

    @pl.when(ki == 0)
    def _():
        m_scr[...] = jnp.full_like(m_scr, NEG_INF)
        acc_scr[...] = jnp.zeros_like(acc_scr)
        lane = lax.broadcasted_iota(I32, (tk, HEAD_DIM), 1)
        va_scr[:, HEAD_DIM:] = jnp.where(lane == 0, 1.0, 0.0).astype(BF16)

    va_scr[:, :HEAD_DIM] = v_ref[...]
    k = k_ref[...]
    va = va_scr[...]
    for gq in range(groups):
        q = q_ref[:, gq * HEAD_DIM:(gq + 1) * HEAD_DIM]
        s = _chunked(lambda x: _dot_nt(x, k), q, FLASH_DOT_ROWS)
        m_prev = m_scr[gq][:, :1]
        m_next = jnp.maximum(m_prev, jnp.max(s, axis=1, keepdims=True))
        alpha = jnp.exp2(m_prev - m_next)
        p = jnp.exp2(s - m_next).astype(BF16)
        acc_scr[gq] = alpha * acc_scr[gq] + _chunked(lambda x: _dot(x, va), p, FLASH_DOT_ROWS)
        m_scr[gq] = jnp.broadcast_to(m_next, m_scr.shape[1:])

    @pl.when(ki == pl.num_programs(3) - 1)
    def _():
        for gq in range(groups):
            acc = acc_scr[gq]
            o_ref[:, gq * HEAD_DIM:(gq + 1) * HEAD_DIM] = (
                acc[:, :HEAD_DIM] / acc[:, HEAD_DIM:HEAD_DIM + 1]).astype(BF16)


def _flash_group(proj, row0, batch, t):
    tq, tk = min(FLASH_TQ, t), min(FLASH_TK, t)
    assert row0 % tq == 0 and row0 % tk == 0 and t % tq == 0 and t % tk == 0
    kv_heads = GA_KV_W // HEAD_DIM
    qw = GA_W // kv_heads
    groups = qw // HEAD_DIM
    qcol0 = (4 * RET_W + 3 * NA_W) // qw
    kcol0 = (4 * RET_W + 3 * NA_W + GA_W) // HEAD_DIM
    vcol0 = kcol0 + kv_heads
    return pl.pallas_call(
        _flash_kernel,
        out_shape=jax.ShapeDtypeStruct((batch * t, GA_W), BF16),
        grid=(batch, kv_heads, t // tq, t // tk),
        in_specs=[
            pl.BlockSpec((tq, qw), lambda b, h, qi, ki: ((row0 + b * t) // tq + qi, qcol0 + h)),
            pl.BlockSpec((tk, HEAD_DIM), lambda b, h, qi, ki: ((row0 + b * t) // tk + ki, kcol0 + h)),
            pl.BlockSpec((tk, HEAD_DIM), lambda b, h, qi, ki: ((row0 + b * t) // tk + ki, vcol0 + h)),
        ],
        out_specs=pl.BlockSpec((tq, qw), lambda b, h, qi, ki: ((b * t) // tq + qi, h)),
        scratch_shapes=[pltpu.VMEM((groups, tq, HEAD_DIM), F32),
                        pltpu.VMEM((groups, tq, 2 * HEAD_DIM), F32),
                        pltpu.VMEM((tk, 2 * HEAD_DIM), BF16)],
        compiler_params=_cparams(("parallel", "parallel", "parallel", "arbitrary")),
        name="flash_gqa",
    )(proj, proj, proj)


def _outproj_kernel(yr_ref, yn_ref, yg_ref, x_ref, w_ref, non_ref, gon_ref, gt_ref, sc_ref, sh_ref, g2_ref,
                    x1_ref, h2_ref, h2p_ref):
    ynn = _rms(yn_ref[...].astype(F32), non_ref[...]).astype(BF16)
    ygn = _rms(yg_ref[...].astype(F32), gon_ref[...]).astype(BF16)
    acc = _dot(yr_ref[...], w_ref[0:RET_W, :])
    acc = acc + _dot(ynn, w_ref[RET_W:RET_W + NA_W, :])
    acc = acc + _dot(ygn, w_ref[RET_W + NA_W:, :])
    x1 = x_ref[...] + gt_ref[...] * acc
    x1_ref[...] = x1
    h2 = _rms(x1, g2_ref[...]) * (1.0 + sc_ref[...]) + sh_ref[...]
    h2_ref[...] = h2
    h2p_ref[...] = _pack_bf16_pairs(h2)


def _pack_bf16_pairs(x):
    w = x.shape[1] // 2
    bits = pltpu.bitcast(x.astype(BF16).astype(F32), jnp.uint32)
    return (bits[:, w:] & jnp.uint32(0xFFFF0000)) | (bits[:, :w] >> 16)


def _unpack_bf16_pairs(p):
    lo = pltpu.bitcast(p << 16, F32)
    hi = pltpu.bitcast(p & jnp.uint32(0xFFFF0000), F32)
    return jnp.concatenate([lo, hi], axis=1).astype(BF16)


def _outproj(y_ret, y_na, y_ga, x, w_out_bf, na_on, ga_on, mod5, layer, g2, geo):
    n = x.shape[0]
    tm = OUT_TM

    def mod_spec(chunk):
        def f(i):
            b, _ = geo.locate(i * tm)
            return (layer, chunk, b, 0, 0)
        return pl.BlockSpec((None, None, None, 1, D_MODEL), f)

    row = lambda w: pl.BlockSpec((tm, w), lambda i: (i, 0))
    return pl.pallas_call(
        _outproj_kernel,
        out_shape=(jax.ShapeDtypeStruct((n, D_MODEL), F32), jax.ShapeDtypeStruct((n, D_MODEL), F32),
                   jax.ShapeDtypeStruct((n, D_MODEL // 2), jnp.uint32)),
        grid=(n // tm,),
        in_specs=[row(RET_W), row(NA_W), row(GA_W), row(D_MODEL),
                  pl.BlockSpec((D_MODEL, D_MODEL), lambda i: (0, 0)),
                  pl.BlockSpec((1, NA_W), lambda i: (0, 0)),
                  pl.BlockSpec((1, GA_W), lambda i: (0, 0)),
                  mod_spec(2), mod_spec(4), mod_spec(3),
                  pl.BlockSpec((1, D_MODEL), lambda i: (0, 0))],
        out_specs=(row(D_MODEL), row(D_MODEL), row(D_MODEL // 2)),
        compiler_params=_cparams(("parallel",)),
        name="outproj",
    )(y_ret, y_na, y_ga, x, w_out_bf, na_on, ga_on, mod5, mod5, mod5, g2)


def _first_index_of_max(vals, ids, axes, sentinel):
    m = vals
    for ax in axes:
        m = jnp.max(m, axis=ax, keepdims=True)
    cand = jnp.where(vals == m, ids, sentinel)
    for ax in axes:
        cand = jnp.min(cand, axis=ax, keepdims=True)
    return m, cand


def _router_kernel(h_ref, w_ref, b_ref, idx_ref, wgt_ref, rank_ref, cnt_ref, cnt_scr, tri_scr):
    i = pl.program_id(0)
    tm = h_ref.shape[0]

    @pl.when(i == 0)
    def _():
        cnt_scr[...] = jnp.zeros_like(cnt_scr)
        r = lax.broadcasted_iota(I32, (tm, tm), 0)
        c = lax.broadcasted_iota(I32, (tm, tm), 1)
        tri_scr[...] = jnp.where(r < c, 1.0, 0.0).astype(BF16)

    logits = lax.dot_general(w_ref[...], h_ref[...], (((1,), (1,)), ((), ())),
                             preferred_element_type=F32, precision=lax.Precision.HIGHEST)
    scores = _sigmoid(logits)
    choice = scores + b_ref[...][:, :1]
    shape3 = (N_GROUPS, GROUP_SIZE, tm)
    choice3 = choice.reshape(shape3)
    scores3 = scores.reshape(shape3)
    sub = lax.broadcasted_iota(I32, shape3, 1)
    grp = lax.broadcasted_iota(I32, shape3, 0)
    eid = grp * GROUP_SIZE + sub

    m1, i1 = _first_index_of_max(choice3, sub, (1,), GROUP_SIZE)
    rest = jnp.where(sub == i1, NEG_INF, choice3)
    m2 = jnp.max(rest, axis=1, keepdims=True)
    gscore = m1 + m2

    gid = lax.broadcasted_iota(I32, gscore.shape, 0)
    gsel = jnp.zeros(gscore.shape, F32)
    for _ in range(TOPK_GROUPS):
        _, gi = _first_index_of_max(gscore, gid, (0,), N_GROUPS)
        hit = gid == gi
        gsel = jnp.where(hit, 1.0, gsel)
        gscore = jnp.where(hit, NEG_INF, gscore)

    masked = jnp.where(gsel > 0.0, choice3, NEG_INF)
    onehots, ids, ws = [], [], []
    for _ in range(TOP_K):
        _, ei = _first_index_of_max(masked, eid, (1, 0), N_EXPERTS)
        hit = eid == ei
        onehots.append(hit)
        ids.append(ei.reshape(1, tm))
        ws.append(jnp.sum(jnp.where(hit, scores3, 0.0), axis=(0, 1), keepdims=True).reshape(1, tm))
        masked = jnp.where(hit, NEG_INF, masked)
    wsum = ws[0]
    for k in range(1, TOP_K):
        wsum = wsum + ws[k]

    sel = jnp.zeros(shape3, F32)
    for hit in onehots:
        sel = jnp.where(hit, 1.0, sel)
    sel2 = sel.reshape(N_EXPERTS, tm)
    before = _dot(sel2.astype(BF16), tri_scr[...]) + cnt_scr[:, :1]
    before3 = before.reshape(shape3)
    ranks = [jnp.sum(jnp.where(hit, before3, 0.0), axis=(0, 1), keepdims=True).reshape(1, tm)
             for hit in onehots]
    cnt_scr[...] = cnt_scr[...] + jnp.sum(sel2, axis=1, keepdims=True)

    idx_ref[...] = jnp.concatenate(ids, axis=0)
    wgt_ref[...] = jnp.concatenate([w / wsum * ROUTED_SCALE for w in ws], axis=0)
    rank_ref[...] = jnp.concatenate(ranks, axis=0).astype(I32)
    cnt_ref[...] = cnt_scr[...]


def _router(h2, w_router_t, bias_col):
    n = h2.shape[0]
    tm = ROUTER_TM
    out_blk = pl.BlockSpec((TOP_K, tm), lambda i: (0, i))
    return pl.pallas_call(
        _router_kernel,
        out_shape=(jax.ShapeDtypeStruct((TOP_K, n), I32), jax.ShapeDtypeStruct((TOP_K, n), F32),
                   jax.ShapeDtypeStruct((TOP_K, n), I32), jax.ShapeDtypeStruct((N_EXPERTS, HEAD_DIM), F32)),
        grid=(n // tm,),
        in_specs=[pl.BlockSpec((tm, D_MODEL), lambda i: (i, 0)),
                  pl.BlockSpec((N_EXPERTS, D_MODEL), lambda i: (0, 0)),
                  pl.BlockSpec((N_EXPERTS, HEAD_DIM), lambda i: (0, 0))],
        out_specs=(out_blk, out_blk, out_blk, pl.BlockSpec((N_EXPERTS, HEAD_DIM), lambda i: (0, 0))),
        scratch_shapes=[pltpu.VMEM((N_EXPERTS, HEAD_DIM), F32), pltpu.VMEM((tm, tm), BF16)],
        compiler_params=_cparams(("arbitrary",)),
        name="router",
    )(h2, w_router_t, bias_col)


def _row_copy(src, src_row, dst, dst_row, sem, rows=1):
    return pltpu.make_async_copy(src.at[pl.ds(src_row, rows)], dst.at[pl.ds(dst_row, rows)], sem)


def _zero_fill_padding(fill_ref, end_ref, xs_hbm, zero_scr, zsem, wait):
    def go(copy):
        copy.wait() if wait else copy.start()

    def per_expert(e, carry):
        def per_row(r, c):
            go(pltpu.make_async_copy(zero_scr.at[pl.ds(0, 1)], xs_hbm.at[pl.ds(r, 1)], zsem))
            return c
        return lax.fori_loop(fill_ref[e], end_ref[e], per_row, carry)

    lax.fori_loop(0, N_EXPERTS, per_expert, 0)
    total = end_ref[N_EXPERTS - 1]
    n_tiles = xs_hbm.shape[0] // FFN_TM
    for tile in range(n_tiles - N_EXPERTS, n_tiles):
        @pl.when(tile * FFN_TM >= total)
        def _(tile=tile):
            go(pltpu.make_async_copy(zero_scr, xs_hbm.at[pl.ds(tile * FFN_TM, FFN_TM)], zsem))


def _step_slots(pos, tm):
    n = pos.shape[1]
    return pos.T.reshape(n // tm, 1, tm * TOP_K)


def _dispatch_kernel(fill_ref, end_ref, pos_ref, h_ref, xs_hbm, zero_scr, sem, zsem):
    i = pl.program_id(0)
    tm = h_ref.shape[0]

    @pl.when(i == 0)
    def _():
        zero_scr[...] = jnp.zeros_like(zero_scr)
        _zero_fill_padding(fill_ref, end_ref, xs_hbm, zero_scr, zsem, wait=False)
        _zero_fill_padding(fill_ref, end_ref, xs_hbm, zero_scr, zsem, wait=True)

    def issue(t, carry):
        for k in range(TOP_K):
            _row_copy(h_ref, t, xs_hbm, pos_ref[0, 0, t * TOP_K + k], sem).start()
        return carry

    lax.fori_loop(0, tm, issue, 0)
    def drain(t, carry):
        _row_copy(h_ref, 0, xs_hbm, 0, sem, rows=TOP_K).wait()
        return carry

    lax.fori_loop(0, tm, drain, 0)


def _dispatch(h2p, pos, seg_fill, seg_end, n_slots):
    n, width = h2p.shape
    tm = DISPATCH_TM
    smem_blk = pl.BlockSpec((1, 1, TOP_K * tm), lambda i, *_: (i, 0, 0), memory_space=pltpu.SMEM)
    return pl.pallas_call(
        _dispatch_kernel,
        out_shape=jax.ShapeDtypeStruct((n_slots, width), h2p.dtype),
        grid_spec=pltpu.PrefetchScalarGridSpec(
            num_scalar_prefetch=2,
            grid=(n // tm,),
            in_specs=[smem_blk, pl.BlockSpec((tm, width), lambda i, *_: (i, 0))],
            out_specs=pl.BlockSpec(memory_space=pl.ANY),
            scratch_shapes=[pltpu.VMEM((FFN_TM, width), h2p.dtype),
                            pltpu.SemaphoreType.DMA(()), pltpu.SemaphoreType.DMA(())],
        ),
        compiler_params=_cparams(("arbitrary",)),
        name="moe_dispatch",
    )(seg_fill, seg_end, _step_slots(pos, tm), h2p)


def _ffn_kernel(te_ref, nv_ref, x_ref, wg_ref, wu_ref, wd_ref, o_ref, wg_bf, wu_bf, wd_bf):
    i = pl.program_id(0)
    nvalid = nv_ref[i]

    @pl.when((i == 0) | (te_ref[i] != te_ref[jnp.maximum(i - 1, 0)]))
    def _():
        wg_bf[...] = wg_ref[...].astype(BF16)
        wu_bf[...] = wu_ref[...].astype(BF16)
        wd_bf[...] = wd_ref[...].astype(BF16)

    @pl.when(nvalid > 0)
    def _():
        x = _unpack_bf16_pairs(x_ref[...])
        hg = _dot(x, wg_bf[...])
        hu = _dot(x, wu_bf[...])
        h = (hg * _sigmoid(hg) * hu).astype(BF16)
        o_ref[...] = _dot(h, wd_bf[...])

    @pl.when(nvalid == 0)
    def _():
        o_ref[...] = jnp.zeros_like(o_ref)


def _expert_ffn(x_sorted, tile_expert, tile_valid, wg, wu, wd, layer):
    n_slots = x_sorted.shape[0]
    tm = FFN_TM
    return pl.pallas_call(
        _ffn_kernel,
        out_shape=jax.ShapeDtypeStruct((n_slots, D_MODEL), F32),
        grid_spec=pltpu.PrefetchScalarGridSpec(
            num_scalar_prefetch=2,
            grid=(n_slots // tm,),
            in_specs=[pl.BlockSpec((tm, D_MODEL // 2), lambda i, te, nv: (i, 0)),
                      pl.BlockSpec((None, None, D_MODEL, EXPERT_FF), lambda i, te, nv: (layer, te[i], 0, 0)),
                      pl.BlockSpec((None, None, D_MODEL, EXPERT_FF), lambda i, te, nv: (layer, te[i], 0, 0)),
                      pl.BlockSpec((None, None, EXPERT_FF, D_MODEL), lambda i, te, nv: (layer, te[i], 0, 0))],
            out_specs=pl.BlockSpec((tm, D_MODEL), lambda i, te, nv: (i, 0)),
            scratch_shapes=[pltpu.VMEM((D_MODEL, EXPERT_FF), BF16), pltpu.VMEM((D_MODEL, EXPERT_FF), BF16),
                            pltpu.VMEM((EXPERT_FF, D_MODEL), BF16)],
        ),
        compiler_params=_cparams(("arbitrary",)),
        name="moe_ffn",
    )(tile_expert, tile_valid, x_sorted, wg, wu, wd)


def _combine_kernel(pos_ref, posn_ref, wgt_ref, x1_ref, h_ref, gt_ref,
                    wsg_ref, wsu_ref, wsd_ref, y_hbm, o_ref, ybuf, sems):
    i = pl.program_id(0)
    nsteps = pl.num_programs(0)
    tm = x1_ref.shape[0]

    def issue(slot, pr):
        def body(t, carry):
            for k in range(TOP_K):
                pltpu.make_async_copy(y_hbm.at[pl.ds(pr[0, 0, t * TOP_K + k], 1)],
                                      ybuf.at[slot, k, pl.ds(t, 1)], sems.at[slot]).start()
            return carry
        lax.fori_loop(0, tm, body, 0)

    slot = i % 2

    @pl.when(i == 0)
    def _():
        issue(0, pos_ref)

    for nxt in range(2):
        @pl.when((i + 1 < nsteps) & (slot != nxt))
        def _(nxt=nxt):
            issue(nxt, posn_ref)

    hb = h_ref[...].astype(BF16)
    hg = _dot(hb, wsg_ref[...])
    hu = _dot(hb, wsu_ref[...])
    shared = _dot((hg * _sigmoid(hg) * hu).astype(BF16), wsd_ref[...])

    for k in range(TOP_K):
        pltpu.make_async_copy(y_hbm.at[pl.ds(0, tm)], ybuf.at[slot, k], sems.at[slot]).wait()

    acc = shared
    for k in range(TOP_K):
        acc = acc + wgt_ref[:, k:k + 1] * ybuf[slot, k]
    o_ref[...] = x1_ref[...] + gt_ref[...] * acc


def _combine(y_sorted, pos, wgt_t, x1, h2, mod5, layer, wsg, wsu, wsd, geo):
    n = x1.shape[0]
    tm = COMBINE_TM
    nsteps = n // tm
    smem_cur = pl.BlockSpec((1, 1, TOP_K * tm), lambda i: (i, 0, 0), memory_space=pltpu.SMEM)
    smem_next = pl.BlockSpec((1, 1, TOP_K * tm), lambda i: (jnp.minimum(i + 1, nsteps - 1), 0, 0),
                             memory_space=pltpu.SMEM)
    slots = _step_slots(pos, tm)

    def gt_map(i):
        b, _ = geo.locate(i * tm)
        return (layer, 5, b, 0, 0)

    row = pl.BlockSpec((tm, D_MODEL), lambda i: (i, 0))
    whole = lambda a, b: pl.BlockSpec((a, b), lambda i: (0, 0))
    return pl.pallas_call(
        _combine_kernel,
        out_shape=jax.ShapeDtypeStruct((n, D_MODEL), F32),
        grid=(nsteps,),
        in_specs=[smem_cur, smem_next,
                  pl.BlockSpec((tm, TOP_K), lambda i: (i, 0)),
                  row, row,
                  pl.BlockSpec((None, None, None, 1, D_MODEL), gt_map),
                  whole(D_MODEL, EXPERT_FF), whole(D_MODEL, EXPERT_FF), whole(EXPERT_FF, D_MODEL),
                  pl.BlockSpec(memory_space=pl.ANY)],
        out_specs=row,
        scratch_shapes=[pltpu.VMEM((2, TOP_K, tm, D_MODEL), F32), pltpu.SemaphoreType.DMA((2,))],
        compiler_params=_cparams(("arbitrary",)),
        name="moe_combine",
    )(slots, slots, wgt_t, x1, h2, mod5, wsg, wsu, wsd, y_sorted)


def _rope_tables(t_max):
    pos = jnp.arange(t_max, dtype=jnp.int32)

    def angles(p, n_pairs):
        inv = ROPE_BASE ** (-jnp.arange(n_pairs, dtype=F32) / n_pairs)
        ang = p.astype(F32)[:, None] * inv[None, :]
        return jnp.cos(ang), jnp.sin(ang)

    c, s = angles(pos, HEAD_DIM // 2)
    cr, sr = angles(pos // GRID_W, HEAD_DIM // 4)
    cc, sc = angles(pos % GRID_W, HEAD_DIM // 4)
    return (jnp.concatenate([c, c], -1), jnp.concatenate([-s, s], -1),
            jnp.concatenate([cr, cr, cc, cc], -1), jnp.concatenate([-sr, sr, -sc, sc], -1))


def _moe_plan(counts, n_slots):
    padded = (counts + FFN_TM - 1) // FFN_TM * FFN_TM
    seg_end = jnp.cumsum(padded)
    seg_start = seg_end - padded
    tile_row0 = jnp.arange(n_slots // FFN_TM, dtype=I32) * FFN_TM
    tile_expert = jnp.minimum(jnp.sum(tile_row0[:, None] >= seg_end[None, :], axis=1), N_EXPERTS - 1).astype(I32)
    tile_valid = jnp.clip(seg_start[tile_expert] + counts[tile_expert] - tile_row0, 0, FFN_TM).astype(I32)
    return (seg_start.astype(I32), (seg_start + counts).astype(I32), seg_end.astype(I32),
            tile_expert, tile_valid)


def _layer(x, mod5, layer, p, tabs, geo):
    n = x.shape[0]
    proj = _inproj(x, mod5, layer, p["g1"], p["w_in"], tabs, p["gains"], geo)
    y_ret = _retention(proj, p["lg_f"], p["lg_b"], p["ret_gn"], geo)
    y_na = _neighbourhood(proj, p["na_tab"], geo)
    y_ga = jnp.concatenate([_flash_group(proj, 0, geo.bp, geo.tp),
                            _flash_group(proj, geo.n_p, geo.bs, geo.ts)], axis=0)
    x1, h2, h2p = _outproj(y_ret, y_na, y_ga, x, p["w_out"], p["na_on"], p["ga_on"], mod5, layer, p["g2"], geo)
    idx, wgt, rank, cnt = _router(h2, p["w_router_t"], p["router_bias"])
    n_slots = n * TOP_K + N_EXPERTS * FFN_TM
    seg_start, seg_fill, seg_end, tile_expert, tile_valid = _moe_plan(cnt[:, 0].astype(I32), n_slots)
    pos = rank
    for e in range(N_EXPERTS):
        pos = pos + jnp.where(idx == e, seg_start[e], 0)
    x_sorted = _dispatch(h2p, pos, seg_fill, seg_end, n_slots)
    y_sorted = _expert_ffn(x_sorted, tile_expert, tile_valid, p["w_eg"], p["w_eu"], p["w_ed"], layer)
    return _combine(y_sorted, pos, wgt.T, x1, h2, mod5, layer, p["w_sg"], p["w_su"], p["w_sd"], geo)


def kernel(x_prompt, x_sample, c_prompt, c_sample, w_ada, b_ada, norm1, w_in, ret_decay_fwd, ret_decay_bwd, ret_norm, na_q_norm, na_k_norm, na_rpb, na_out_norm, ga_q_norm, ga_k_norm, ga_out_norm, w_out, norm2, w_router, router_bias, w_exp_gate, w_exp_up, w_exp_down, w_sh_gate, w_sh_up, w_sh_down):
    bp, tp, d = x_prompt.shape
    bs, ts, _ = x_sample.shape
    depth = w_ada.shape[0]
    geo = Geom(bp, tp, bs, ts)
    assert d == D_MODEL and w_in.shape[-1] == PROJ_W
    for t in (tp, ts):
        assert t % max(PROJ_TM, FLASH_TQ, NA_SLAB_ROWS * GRID_W) == 0 and t % min(FLASH_TK, t) == 0

    x = jnp.concatenate([x_prompt.reshape(bp * tp, d), x_sample.reshape(bs * ts, d)], axis=0)
    c_all = jnp.concatenate([c_prompt, c_sample], axis=0)
    c_pad = jnp.zeros((8 * pl.cdiv(geo.nb, 8), d), F32).at[:geo.nb].set(c_all)
    mod = _adaln(c_pad, w_ada, b_ada)
    rows = mod.shape[1]
    mod5 = mod.reshape(depth, rows, N_MOD, 1, d).transpose(0, 2, 1, 3, 4)
    tabs = _rope_tables(max(tp, ts))

    for l in range(depth):
        p = {
            "g1": norm1[l].reshape(1, d),
            "w_in": w_in[l].astype(BF16),
            "gains": jnp.stack([na_q_norm[l], na_k_norm[l], ga_q_norm[l], ga_k_norm[l]]),
            "lg_f": jax.nn.log_sigmoid(ret_decay_fwd[l].astype(F32)),
            "lg_b": jax.nn.log_sigmoid(ret_decay_bwd[l].astype(F32)),
            "ret_gn": ret_norm[l].reshape(1, RET_W),
            "na_tab": _na_bias_table(na_rpb[l]),
            "na_on": na_out_norm[l].reshape(1, NA_W),
            "ga_on": ga_out_norm[l].reshape(1, GA_W),
            "w_out": w_out[l].astype(BF16),
            "g2": norm2[l].reshape(1, d),
            "w_router_t": w_router[l].astype(F32).T,
            "router_bias": jnp.broadcast_to(router_bias[l].astype(F32)[:, None], (N_EXPERTS, HEAD_DIM)),
            "w_eg": w_exp_gate,
            "w_eu": w_exp_up,
            "w_ed": w_exp_down,
            "w_sg": w_sh_gate[l].astype(BF16),
            "w_su": w_sh_up[l].astype(BF16),
            "w_sd": w_sh_down[l].astype(BF16),
        }
        x = _layer(x, mod5, l, p, tabs, geo)

    y_prompt = x[:geo.n_p].reshape(bp, tp, d)
    y_sample = x[geo.n_p:].reshape(bs, ts, d)
    return (y_prompt, y_sample)
```

```python
import dataclasses
import functools

import numpy as np
import jax
import jax.numpy as jnp
from jax import lax
from jax.experimental import pallas as pl
from jax.experimental.pallas import tpu as pltpu

F32 = jnp.float32
BF16 = jnp.bfloat16
I32 = jnp.int32

D_MODEL = 2048
HEAD_DIM = 128
RET_W = 512
NA_W = 512
GA_W = 1024
GA_KV_W = 256
PROJ_W = 4 * RET_W + 3 * NA_W + GA_W + 2 * GA_KV_W
RET_CHUNK = 128
GRID_W = 64
NA_WIN_R = 8
NA_WIN_C = 16
ROPE_BASE = 10000.0
N_EXPERTS = 64
N_GROUPS = 8
GROUP_SIZE = N_EXPERTS // N_GROUPS
TOPK_GROUPS = 4
TOP_K = 8
EXPERT_FF = 512
ROUTED_SCALE = 2.5
N_MOD = 6
EPS = 1e-6
QK_SCALE = HEAD_DIM ** -0.5
LOG2E = 1.4426950408889634
NEG_INF = float("-inf")
MASK_VALUE = -1e30

VMEM_LIMIT_BYTES = 56 * 1024 * 1024

PROJ_TN = 512
PROJ_TM = 1024
NA_QROWS = 8
NA_SLAB_ROWS = 16
NA_KBLK_ROWS = 4
NA_DOT_ROWS = 256
FLASH_TQ = 512
FLASH_TK = 8192
FLASH_DOT_ROWS = 256
OUT_TM = 256
ROUTER_TM = 512
DISPATCH_TM = 256
FFN_TM = 512
COMBINE_TM = 128
ADALN_TN = 1024


@dataclasses.dataclass(frozen=True)
class Geom:
    bp: int
    tp: int
    bs: int
    ts: int

    @property
    def n_p(self):
        return self.bp * self.tp

    @property
    def n(self):
        return self.bp * self.tp + self.bs * self.ts

    @property
    def nb(self):
        return self.bp + self.bs

    def locate(self, row):
        in_p = row < self.n_p
        rs = jnp.maximum(row - self.n_p, 0)
        b = jnp.where(in_p, row // self.tp, self.bp + rs // self.ts)
        pos = jnp.where(in_p, row % self.tp, rs % self.ts)
        return b, pos

    def seq_len(self, row):
        return jnp.where(row < self.n_p, self.tp, self.ts)


def _cparams(sem, vmem=VMEM_LIMIT_BYTES):
    return pltpu.CompilerParams(dimension_semantics=sem, vmem_limit_bytes=vmem)


def _dot(a, b):
    return jnp.dot(a, b, preferred_element_type=F32)


def _dot_nt(a, b):
    return lax.dot_general(a, b, (((1,), (1,)), ((), ())), preferred_element_type=F32)


def _sigmoid(x):
    return 1.0 / (1.0 + jnp.exp(-x))


def _rms(x, g):
    return x * lax.rsqrt(jnp.mean(x * x, axis=-1, keepdims=True) + EPS) * g


def _adaln_kernel(c_ref, w_ref, b_ref, o_ref):
    c = c_ref[...]
    s = c * _sigmoid(c)
    o_ref[...] = jnp.dot(s, w_ref[...], preferred_element_type=F32,
                         precision=lax.Precision.HIGHEST) + b_ref[...]


def _adaln(c_pad, w_ada, b_ada):
    depth, d, w = w_ada.shape
    rows = c_pad.shape[0]
    return pl.pallas_call(
        _adaln_kernel,
        out_shape=jax.ShapeDtypeStruct((depth, rows, w), F32),
        grid=(depth, w // ADALN_TN),
        in_specs=[
            pl.BlockSpec((rows, d), lambda l, j: (0, 0)),
            pl.BlockSpec((None, d, ADALN_TN), lambda l, j: (l, 0, j)),
            pl.BlockSpec((None, 1, ADALN_TN), lambda l, j: (l, 0, j)),
        ],
        out_specs=pl.BlockSpec((None, rows, ADALN_TN), lambda l, j: (l, 0, j)),
        compiler_params=_cparams(("parallel", "parallel")),
        name="adaln",
    )(c_pad, w_ada, b_ada.reshape(depth, 1, w))


def _rope_ret(x, cos, sin):
    return x * cos + pltpu.roll(x, HEAD_DIM // 2, 1) * sin


def _rope_axial(x, cos, sin):
    lane = lax.broadcasted_iota(I32, x.shape, 1)
    first = (lane & (HEAD_DIM // 4)) == 0
    rot = jnp.where(first, pltpu.roll(x, HEAD_DIM - HEAD_DIM // 4, 1), pltpu.roll(x, HEAD_DIM // 4, 1))
    return x * cos + rot * sin


def _inproj_kernel(x_ref, sc_ref, sh_ref, g1_ref, w_ref, rc_ref, rs_ref, ac_ref, as_ref, gains_ref,
                   o_ref, h_scr, acc_scr):
    j = pl.program_id(1)

    @pl.when(j == 0)
    def _():
        h = _rms(x_ref[...], g1_ref[...]) * (1.0 + sc_ref[...]) + sh_ref[...]
        h_scr[...] = h.astype(BF16)

    acc_scr[...] = _dot(h_scr[...], w_ref[...])
    heads = PROJ_TN // HEAD_DIM

    def head(hh):
        return acc_scr[:, hh * HEAD_DIM:(hh + 1) * HEAD_DIM]

    def put(hh, val):
        o_ref[:, hh * HEAD_DIM:(hh + 1) * HEAD_DIM] = val.astype(BF16)

    @pl.when(j == 0)
    def _():
        for hh in range(heads):
            put(hh, _rope_ret(head(hh), rc_ref[...], rs_ref[...]))

    @pl.when(j == 1)
    def _():
        for hh in range(heads):
            put(hh, _rope_ret(head(hh), rc_ref[...], rs_ref[...]) * QK_SCALE)

    @pl.when((j == 2) | (j == 6))
    def _():
        o_ref[...] = acc_scr[...].astype(BF16)

    @pl.when(j == 3)
    def _():
        a = acc_scr[...]
        o_ref[...] = (a * _sigmoid(a)).astype(BF16)

    @pl.when(j == 4)
    def _():
        for hh in range(heads):
            put(hh, _rms(head(hh), gains_ref[0:1, :]))

    @pl.when(j == 5)
    def _():
        for hh in range(heads):
            put(hh, _rms(head(hh), gains_ref[1:2, :]))

    @pl.when((j == 7) | (j == 8))
    def _():
        for hh in range(heads):
            put(hh, _rope_axial(_rms(head(hh), gains_ref[2:3, :]), ac_ref[...], as_ref[...])
                * (QK_SCALE * LOG2E))

    @pl.when(j == 9)
    def _():
        for hh in range(2):
            put(hh, _rope_axial(_rms(head(hh), gains_ref[3:4, :]), ac_ref[...], as_ref[...]))
        o_ref[:, 2 * HEAD_DIM:] = acc_scr[:, 2 * HEAD_DIM:].astype(BF16)


def _inproj(x, mod5, layer, g1, w_in_bf, tabs, gains, geo):
    n = x.shape[0]
    tm = PROJ_TM

    def mod_map(chunk):
        def f(i, j):
            b, _ = geo.locate(i * tm)
            return (layer, chunk, b, 0, 0)
        return f

    def tab_map(i, j):
        _, pos = geo.locate(i * tm)
        return (pos // tm, 0)

    mod_spec = lambda chunk: pl.BlockSpec((None, None, None, 1, D_MODEL), mod_map(chunk))
    tab_spec = pl.BlockSpec((tm, HEAD_DIM), tab_map)
    return pl.pallas_call(
        _inproj_kernel,
        out_shape=jax.ShapeDtypeStruct((n, PROJ_W), BF16),
        grid=(n // tm, PROJ_W // PROJ_TN),
        in_specs=[
            pl.BlockSpec((tm, D_MODEL), lambda i, j: (i, 0)),
            mod_spec(1), mod_spec(0),
            pl.BlockSpec((1, D_MODEL), lambda i, j: (0, 0)),
            pl.BlockSpec((D_MODEL, PROJ_TN), lambda i, j: (0, j)),
            tab_spec, tab_spec, tab_spec, tab_spec,
            pl.BlockSpec((4, HEAD_DIM), lambda i, j: (0, 0)),
        ],
        out_specs=pl.BlockSpec((tm, PROJ_TN), lambda i, j: (i, j)),
        scratch_shapes=[pltpu.VMEM((tm, D_MODEL), BF16), pltpu.VMEM((tm, PROJ_TN), F32)],
        compiler_params=_cparams(("parallel", "arbitrary")),
        name="inproj",
    )(x, mod5, mod5, g1, w_in_bf, tabs[0], tabs[1], tabs[2], tabs[3], gains)


def _ret_kernel(lg_ref, q_ref, k_ref, v_ref, *rest, reverse, geo):
    if reverse:
        o_ref, s_scr, tab_scr = rest
    else:
        sg_ref, ob_ref, gn_ref, o_ref, s_scr, tab_scr = rest
    i = pl.program_id(0)
    nchunks = pl.num_programs(0)
    c = RET_CHUNK
    heads = RET_W // HEAD_DIM

    @pl.when(i == 0)
    def _():
        row = lax.broadcasted_iota(I32, (c, c), 0).astype(F32)
        col = lax.broadcasted_iota(I32, (c, c), 1).astype(F32)
        for hh in range(heads):
            lg = lg_ref[hh]
            if reverse:
                diff = col - row
                decay = jnp.where(diff > 0, jnp.exp(lg * jnp.maximum(diff, 0.0)), 0.0)
                qdec = jnp.exp(lg * (c - row))
                kdec = jnp.exp(lg * row)
            else:
                diff = row - col
                decay = jnp.where(diff >= 0, jnp.exp(lg * jnp.maximum(diff, 0.0)), 0.0)
                qdec = jnp.exp(lg * (row + 1.0))
                kdec = jnp.exp(lg * (c - 1.0 - row))
            tab_scr[hh, 0] = decay
            tab_scr[hh, 1] = qdec
            tab_scr[hh, 2] = kdec
            tab_scr[hh, 3] = jnp.exp(jnp.zeros((c, c), F32) + lg * c)

    chunk = (nchunks - 1 - i) if reverse else i
    row0 = chunk * c
    _, pos = geo.locate(row0)
    boundary = (pos + c == geo.seq_len(row0)) if reverse else (pos == 0)

    @pl.when(boundary)
    def _():
        s_scr[...] = jnp.zeros_like(s_scr)

    for hh in range(heads):
        sl = slice(hh * HEAD_DIM, (hh + 1) * HEAD_DIM)
        qh = q_ref[:, sl]
        kh = k_ref[:, sl]
        vh = v_ref[:, sl]
        s = _dot_nt(qh, kh) * tab_scr[hh, 0]
        o = _dot(s.astype(BF16), vh)
        qd = (qh.astype(F32) * tab_scr[hh, 1]).astype(BF16)
        o = o + _dot(qd, s_scr[hh].astype(BF16))
        kd_t = (kh.astype(F32) * tab_scr[hh, 2]).T.astype(BF16)
        s_scr[hh] = s_scr[hh] * tab_scr[hh, 3] + _dot(kd_t, vh)
        if reverse:
            o_ref[:, sl] = o
        else:
            tot = o + ob_ref[:, sl]
            y = _rms(tot, gn_ref[:, sl])
            o_ref[:, sl] = (sg_ref[:, sl].astype(F32) * y).astype(BF16)


def _retention(proj, lg_f, lg_b, ret_gn, geo):
    n = proj.shape[0]
    nchunks = n // RET_CHUNK
    c = RET_CHUNK
    heads = RET_W // HEAD_DIM
    scratch = [pltpu.VMEM((heads, HEAD_DIM, HEAD_DIM), F32), pltpu.VMEM((heads, 4, c, c), F32)]
    smem = pl.BlockSpec(memory_space=pltpu.SMEM)

    def col_spec(colblk, rev):
        if rev:
            return pl.BlockSpec((c, RET_W), lambda i: (nchunks - 1 - i, colblk))
        return pl.BlockSpec((c, RET_W), lambda i: (i, colblk))

    o_bwd = pl.pallas_call(
        functools.partial(_ret_kernel, reverse=True, geo=geo),
        out_shape=jax.ShapeDtypeStruct((n, RET_W), F32),
        grid=(nchunks,),
        in_specs=[smem, col_spec(0, True), col_spec(1, True), col_spec(2, True)],
        out_specs=pl.BlockSpec((c, RET_W), lambda i: (nchunks - 1 - i, 0)),
        scratch_shapes=scratch,
        compiler_params=_cparams(("arbitrary",)),
        name="ret_bwd",
    )(lg_b, proj, proj, proj)
    return pl.pallas_call(
        functools.partial(_ret_kernel, reverse=False, geo=geo),
        out_shape=jax.ShapeDtypeStruct((n, RET_W), BF16),
        grid=(nchunks,),
        in_specs=[smem, col_spec(0, False), col_spec(1, False), col_spec(2, False), col_spec(3, False),
                  pl.BlockSpec((c, RET_W), lambda i: (i, 0)),
                  pl.BlockSpec((1, RET_W), lambda i: (0, 0))],
        out_specs=pl.BlockSpec((c, RET_W), lambda i: (i, 0)),
        scratch_shapes=scratch,
        compiler_params=_cparams(("arbitrary",)),
        name="ret_fwd",
    )(lg_f, proj, proj, proj, proj, o_bwd, ret_gn)


def _na_geometry(i, geo):
    rp, rs = geo.tp // GRID_W, geo.ts // GRID_W
    r0g = i * NA_QROWS
    in_p = r0g < geo.bp * rp
    rsmp = jnp.maximum(r0g - geo.bp * rp, 0)
    rows = jnp.where(in_p, rp, rs)
    lr0 = jnp.where(in_p, r0g % rp, rsmp % rs)
    slab = jnp.clip(lr0 - NA_WIN_R // 2, 0, rows - NA_SLAB_ROWS)
    return rows, lr0, slab, r0g - lr0


def _na_kernel(q_ref, k0, k1, k2, k3, v0, v1, v2, v3, tab_ref, o_ref):
    k = jnp.concatenate([k0[...], k1[...], k2[...], k3[...]], axis=0)
    v = jnp.concatenate([v0[...], v1[...], v2[...], v3[...]], axis=0)
    s = _chunked(lambda x: _dot_nt(x, k), q_ref[...], NA_DOT_ROWS) * QK_SCALE + tab_ref[...]
    e = jnp.exp(s - jnp.max(s, axis=-1, keepdims=True))
    prob = (e / jnp.sum(e, axis=-1, keepdims=True)).astype(BF16)
    o_ref[...] = _chunked(lambda x: _dot(x, v), prob, NA_DOT_ROWS).astype(BF16)


def _na_bias_table(rpb):
    rpb = rpb.astype(F32)
    rows = jnp.stack([rpb[:, NA_WIN_R - 1 - v:2 * NA_WIN_R - 1 - v, :] for v in range(NA_WIN_R)], axis=1)
    width = 2 * GRID_W
    lead = GRID_W - NA_WIN_C
    p = jnp.pad(rows, ((0, 0), (0, 0), (0, 0), (lead, width - lead - (2 * NA_WIN_C - 1))))
    skew = jnp.tile(p, GRID_W)[..., :GRID_W * (width - 1)].reshape(p.shape[:-1] + (GRID_W, width - 1))
    toe = skew[..., GRID_W - 1:]
    c = np.arange(GRID_W)[:, None]
    j = np.arange(GRID_W)[None, :]
    cs = np.clip(c - NA_WIN_C // 2, 0, GRID_W - NA_WIN_C)
    valid = (j >= cs) & (j < cs + NA_WIN_C)
    tab8 = jnp.where(valid[None, None, :, None, :], toe.transpose(0, 1, 3, 2, 4), MASK_VALUE)
    blocks = []
    for d in range(NA_SLAB_ROWS // NA_WIN_R + 1):
        per_row = []
        for p in range(NA_QROWS):
            first = (max(p - NA_WIN_R // 2, 0), p, min(p + NA_WIN_R // 2, NA_WIN_R))[d]
            win = tab8[:, d * (NA_WIN_R // 2) + p - first]
            per_row.append(jnp.pad(win, ((0, 0), (0, 0), (first, NA_SLAB_ROWS - NA_WIN_R - first), (0, 0)),
                                   constant_values=MASK_VALUE))
        blocks.append(jnp.stack(per_row, axis=1))
    return jnp.stack(blocks, axis=1).reshape(rpb.shape[0], len(blocks), NA_QROWS * GRID_W,
                                             NA_SLAB_ROWS * GRID_W)


def _neighbourhood(proj, bias_tab, geo):
    n = proj.shape[0]
    tq = NA_QROWS * GRID_W
    blk = NA_KBLK_ROWS * GRID_W
    nblk = NA_SLAB_ROWS // NA_KBLK_ROWS
    heads = NA_W // HEAD_DIM
    qcol, kcol, vcol = 4 * heads, 5 * heads, 6 * heads

    def kv_spec(col0, m):
        def f(h, i):
            _, _, slab, seq_row0 = _na_geometry(i, geo)
            return ((seq_row0 + slab) // NA_KBLK_ROWS + m, col0 + h)
        return pl.BlockSpec((blk, HEAD_DIM), f)

    def tab_map(h, i):
        _, lr0, slab, _ = _na_geometry(i, geo)
        return (h, (lr0 - slab) // (NA_WIN_R // 2), 0, 0)

    return pl.pallas_call(
        _na_kernel,
        out_shape=jax.ShapeDtypeStruct((n, NA_W), BF16),
        grid=(heads, n // tq),
        in_specs=[pl.BlockSpec((tq, HEAD_DIM), lambda h, i: (i, qcol + h))]
        + [kv_spec(kcol, m) for m in range(nblk)]
        + [kv_spec(vcol, m) for m in range(nblk)]
        + [pl.BlockSpec((None, None, tq, NA_SLAB_ROWS * GRID_W), tab_map)],
        out_specs=pl.BlockSpec((tq, HEAD_DIM), lambda h, i: (i, h)),
        compiler_params=_cparams(("parallel", "parallel")),
        name="natten",
    )(*([proj] * (1 + 2 * nblk)), bias_tab)


def _chunked(fn, x, rows):
    return jnp.concatenate([fn(x[c * rows:(c + 1) * rows]) for c in range(x.shape[0] // rows)], axis=0)


def _flash_kernel(q_ref, k_ref, v_ref, o_ref, m_scr, acc_scr, va_scr):
    ki = pl.program_id(3)
    groups = q_ref.shape[1] // HEAD_DIM
    tk = k_ref.shape[0]

    @pl.when(ki == 0)
    def _():
        m_scr[...] = jnp.full_like(m_scr, NEG_INF)
        acc_scr[...] = jnp.zeros_like(acc_scr)
        lane = lax.broadcasted_iota(I32, (tk, HEAD_DIM), 1)
        va_scr[:, HEAD_DIM:] = jnp.where(lane == 0, 1.0, 0.0).astype(BF16)

    va_scr[:, :HEAD_DIM] = v_ref[...]
    k = k_ref[...]
    va = va_scr[...]
    for gq in range(groups):
        q = q_ref[:, gq * HEAD_DIM:(gq + 1) * HEAD_DIM]
        s = _chunked(lambda x: _dot_nt(x, k), q, FLASH_DOT_ROWS)
        m_prev = m_scr[gq][:, :1]
        m_next = jnp.maximum(m_prev, jnp.max(s, axis=1, keepdims=True))
        alpha = jnp.exp2(m_prev - m_next)
        p = jnp.exp2(s - m_next).astype(BF16)
        acc_scr[gq] = alpha * acc_scr[gq] + _chunked(lambda x: _dot(x, va), p, FLASH_DOT_ROWS)
        m_scr[gq] = jnp.broadcast_to(m_next, m_scr.shape[1:])

    @pl.when(ki == pl.num_programs(3) - 1)
    def _():
        for gq in range(groups):
            acc = acc_scr[gq]
            o_ref[:, gq * HEAD_DIM:(gq + 1) * HEAD_DIM] = (
                acc[:, :HEAD_DIM] / acc[:, HEAD_DIM:HEAD_DIM + 1]).astype(BF16)


def _flash_group(proj, row0, batch, t):
    tq, tk = min(FLASH_TQ, t), min(FLASH_TK, t)
    assert row0 % tq == 0 and row0 % tk == 0 and t % tq == 0 and t % tk == 0
    kv_heads = GA_KV_W // HEAD_DIM
    qw = GA_W // kv_heads
    groups = qw // HEAD_DIM
    qcol0 = (4 * RET_W + 3 * NA_W) // qw
    kcol0 = (4 * RET_W + 3 * NA_W + GA_W) // HEAD_DIM
    vcol0 = kcol0 + kv_heads
    return pl.pallas_call(
        _flash_kernel,
        out_shape=jax.ShapeDtypeStruct((batch * t, GA_W), BF16),
        grid=(batch, kv_heads, t // tq, t // tk),
        in_specs=[
            pl.BlockSpec((tq, qw), lambda b, h, qi, ki: ((row0 + b * t) // tq + qi, qcol0 + h)),
            pl.BlockSpec((tk, HEAD_DIM), lambda b, h, qi, ki: ((row0 + b * t) // tk + ki, kcol0 + h)),
            pl.BlockSpec((tk, HEAD_DIM), lambda b, h, qi, ki: ((row0 + b * t) // tk + ki, vcol0 + h)),
        ],
        out_specs=pl.BlockSpec((tq, qw), lambda b, h, qi, ki: ((b * t) // tq + qi, h)),
        scratch_shapes=[pltpu.VMEM((groups, tq, HEAD_DIM), F32),
                        pltpu.VMEM((groups, tq, 2 * HEAD_DIM), F32),
                        pltpu.VMEM((tk, 2 * HEAD_DIM), BF16)],
        compiler_params=_cparams(("parallel", "parallel", "parallel", "arbitrary")),
        name="flash_gqa",
    )(proj, proj, proj)


def _outproj_kernel(yr_ref, yn_ref, yg_ref, x_ref, w_ref, non_ref, gon_ref, gt_ref, sc_ref, sh_ref, g2_ref,
                    x1_ref, h2_ref, h2p_ref):
    ynn = _rms(yn_ref[...].astype(F32), non_ref[...]).astype(BF16)
    ygn = _rms(yg_ref[...].astype(F32), gon_ref[...]).astype(BF16)
    acc = _dot(yr_ref[...], w_ref[0:RET_W, :])
    acc = acc + _dot(ynn, w_ref[RET_W:RET_W + NA_W, :])
    acc = acc + _dot(ygn, w_ref[RET_W + NA_W:, :])
    x1 = x_ref[...] + gt_ref[...] * acc
    x1_ref[...] = x1
    h2 = _rms(x1, g2_ref[...]) * (1.0 + sc_ref[...]) + sh_ref[...]
    h2_ref[...] = h2
    h2p_ref[...] = _pack_bf16_pairs(h2)


def _pack_bf16_pairs(x):
    w = x.shape[1] // 2
    bits = pltpu.bitcast(x.astype(BF16).astype(F32), jnp.uint32)
    return (bits[:, w:] & jnp.uint32(0xFFFF0000)) | (bits[:, :w] >> 16)


def _unpack_bf16_pairs(p):
    lo = pltpu.bitcast(p << 16, F32)
    hi = pltpu.bitcast(p & jnp.uint32(0xFFFF0000), F32)
    return jnp.concatenate([lo, hi], axis=1).astype(BF16)


def _outproj(y_ret, y_na, y_ga, x, w_out_bf, na_on, ga_on, mod5, layer, g2, geo):
    n = x.shape[0]
    tm = OUT_TM

    def mod_spec(chunk):
        def f(i):
            b, _ = geo.locate(i * tm)
            return (layer, chunk, b, 0, 0)
        return pl.BlockSpec((None, None, None, 1, D_MODEL), f)

    row = lambda w: pl.BlockSpec((tm, w), lambda i: (i, 0))
    return pl.pallas_call(
        _outproj_kernel,
        out_shape=(jax.ShapeDtypeStruct((n, D_MODEL), F32), jax.ShapeDtypeStruct((n, D_MODEL), F32),
                   jax.ShapeDtypeStruct((n, D_MODEL // 2), jnp.uint32)),
        grid=(n // tm,),
        in_specs=[row(RET_W), row(NA_W), row(GA_W), row(D_MODEL),
                  pl.BlockSpec((D_MODEL, D_MODEL), lambda i: (0, 0)),
                  pl.BlockSpec((1, NA_W), lambda i: (0, 0)),
                  pl.BlockSpec((1, GA_W), lambda i: (0, 0)),
                  mod_spec(2), mod_spec(4), mod_spec(3),
                  pl.BlockSpec((1, D_MODEL), lambda i: (0, 0))],
        out_specs=(row(D_MODEL), row(D_MODEL), row(D_MODEL // 2)),
        compiler_params=_cparams(("parallel",)),
        name="outproj",
    )(y_ret, y_na, y_ga, x, w_out_bf, na_on, ga_on, mod5, mod5, mod5, g2)


def _first_index_of_max(vals, ids, axes, sentinel):
    m = vals
    for ax in axes:
        m = jnp.max(m, axis=ax, keepdims=True)
    cand = jnp.where(vals == m, ids, sentinel)
    for ax in axes:
        cand = jnp.min(cand, axis=ax, keepdims=True)
    return m, cand


def _router_kernel(h_ref, w_ref, b_ref, idx_ref, wgt_ref, rank_ref, cnt_ref, cnt_scr, tri_scr):
    i = pl.program_id(0)
    tm = h_ref.shape[0]

    @pl.when(i == 0)
    def _():
        cnt_scr[...] = jnp.zeros_like(cnt_scr)
        r = lax.broadcasted_iota(I32, (tm, tm), 0)
        c = lax.broadcasted_iota(I32, (tm, tm), 1)
        tri_scr[...] = jnp.where(r < c, 1.0, 0.0).astype(BF16)

    logits = lax.dot_general(w_ref[...], h_ref[...], (((1,), (1,)), ((), ())),
                             preferred_element_type=F32, precision=lax.Precision.HIGHEST)
    scores = _sigmoid(logits)
    choice = scores + b_ref[...][:, :1]
    shape3 = (N_GROUPS, GROUP_SIZE, tm)
    choice3 = choice.reshape(shape3)
    scores3 = scores.reshape(shape3)
    sub = lax.broadcasted_iota(I32, shape3, 1)
    grp = lax.broadcasted_iota(I32, shape3, 0)
    eid = grp * GROUP_SIZE + sub

    m1, i1 = _first_index_of_max(choice3, sub, (1,), GROUP_SIZE)
    rest = jnp.where(sub == i1, NEG_INF, choice3)
    m2 = jnp.max(rest, axis=1, keepdims=True)
    gscore = m1 + m2

    gid = lax.broadcasted_iota(I32, gscore.shape, 0)
    gsel = jnp.zeros(gscore.shape, F32)
    for _ in range(TOPK_GROUPS):
        _, gi = _first_index_of_max(gscore, gid, (0,), N_GROUPS)
        hit = gid == gi
        gsel = jnp.where(hit, 1.0, gsel)
        gscore = jnp.where(hit, NEG_INF, gscore)

    masked = jnp.where(gsel > 0.0, choice3, NEG_INF)
    onehots, ids, ws = [], [], []
    for _ in range(TOP_K):
        _, ei = _first_index_of_max(masked, eid, (1, 0), N_EXPERTS)
        hit = eid == ei
        onehots.append(hit)
        ids.append(ei.reshape(1, tm))
        ws.append(jnp.sum(jnp.where(hit, scores3, 0.0), axis=(0, 1), keepdims=True).reshape(1, tm))
        masked = jnp.where(hit, NEG_INF, masked)
    wsum = ws[0]
    for k in range(1, TOP_K):
        wsum = wsum + ws[k]

    sel = jnp.zeros(shape3, F32)
    for hit in onehots:
        sel = jnp.where(hit, 1.0, sel)
    sel2 = sel.reshape(N_EXPERTS, tm)
    before = _dot(sel2.astype(BF16), tri_scr[...]) + cnt_scr[:, :1]
    before3 = before.reshape(shape3)
    ranks = [jnp.sum(jnp.where(hit, before3, 0.0), axis=(0, 1), keepdims=True).reshape(1, tm)
             for hit in onehots]
    cnt_scr[...] = cnt_scr[...] + jnp.sum(sel2, axis=1, keepdims=True)

    idx_ref[...] = jnp.concatenate(ids, axis=0)
    wgt_ref[...] = jnp.concatenate([w / wsum * ROUTED_SCALE for w in ws], axis=0)
    rank_ref[...] = jnp.concatenate(ranks, axis=0).astype(I32)
    cnt_ref[...] = cnt_scr[...]


def _router(h2, w_router_t, bias_col):
    n = h2.shape[0]
    tm = ROUTER_TM
    out_blk = pl.BlockSpec((TOP_K, tm), lambda i: (0, i))
    return pl.pallas_call(
        _router_kernel,
        out_shape=(jax.ShapeDtypeStruct((TOP_K, n), I32), jax.ShapeDtypeStruct((TOP_K, n), F32),
                   jax.ShapeDtypeStruct((TOP_K, n), I32), jax.ShapeDtypeStruct((N_EXPERTS, HEAD_DIM), F32)),
        grid=(n // tm,),
        in_specs=[pl.BlockSpec((tm, D_MODEL), lambda i: (i, 0)),
                  pl.BlockSpec((N_EXPERTS, D_MODEL), lambda i: (0, 0)),
                  pl.BlockSpec((N_EXPERTS, HEAD_DIM), lambda i: (0, 0))],
        out_specs=(out_blk, out_blk, out_blk, pl.BlockSpec((N_EXPERTS, HEAD_DIM), lambda i: (0, 0))),
        scratch_shapes=[pltpu.VMEM((N_EXPERTS, HEAD_DIM), F32), pltpu.VMEM((tm, tm), BF16)],
        compiler_params=_cparams(("arbitrary",)),
        name="router",
    )(h2, w_router_t, bias_col)


def _row_copy(src, src_row, dst, dst_row, sem, rows=1):
    return pltpu.make_async_copy(src.at[pl.ds(src_row, rows)], dst.at[pl.ds(dst_row, rows)], sem)


def _zero_fill_padding(fill_ref, end_ref, xs_hbm, zero_scr, zsem, wait):
    def go(copy):
        copy.wait() if wait else copy.start()

    def per_expert(e, carry):
        def per_row(r, c):
            go(pltpu.make_async_copy(zero_scr.at[pl.ds(0, 1)], xs_hbm.at[pl.ds(r, 1)], zsem))
            return c
        return lax.fori_loop(fill_ref[e], end_ref[e], per_row, carry)

    lax.fori_loop(0, N_EXPERTS, per_expert, 0)
    total = end_ref[N_EXPERTS - 1]
    n_tiles = xs_hbm.shape[0] // FFN_TM
    for tile in range(n_tiles - N_EXPERTS, n_tiles):
        @pl.when(tile * FFN_TM >= total)
        def _(tile=tile):
            go(pltpu.make_async_copy(zero_scr, xs_hbm.at[pl.ds(tile * FFN_TM, FFN_TM)], zsem))


def _step_slots(pos, tm):
    n = pos.shape[1]
    return pos.T.reshape(n // tm, 1, tm * TOP_K)


def _dispatch_kernel(fill_ref, end_ref, pos_ref, h_ref, xs_hbm, zero_scr, sem, zsem):
    i = pl.program_id(0)
    tm = h_ref.shape[0]

    @pl.when(i == 0)
    def _():
        zero_scr[...] = jnp.zeros_like(zero_scr)
        _zero_fill_padding(fill_ref, end_ref, xs_hbm, zero_scr, zsem, wait=False)
        _zero_fill_padding(fill_ref, end_ref, xs_hbm, zero_scr, zsem, wait=True)

    def issue(t, carry):
        for k in range(TOP_K):
            _row_copy(h_ref, t, xs_hbm, pos_ref[0, 0, t * TOP_K + k], sem).start()
        return carry

    lax.fori_loop(0, tm, issue, 0)
    def drain(t, carry):
        _row_copy(h_ref, 0, xs_hbm, 0, sem, rows=TOP_K).wait()
        return carry

    lax.fori_loop(0, tm, drain, 0)


def _dispatch(h2p, pos, seg_fill, seg_end, n_slots):
    n, width = h2p.shape
    tm = DISPATCH_TM
    smem_blk = pl.BlockSpec((1, 1, TOP_K * tm), lambda i, *_: (i, 0, 0), memory_space=pltpu.SMEM)
    return pl.pallas_call(
        _dispatch_kernel,
        out_shape=jax.ShapeDtypeStruct((n_slots, width), h2p.dtype),
        grid_spec=pltpu.PrefetchScalarGridSpec(
            num_scalar_prefetch=2,
            grid=(n // tm,),
            in_specs=[smem_blk, pl.BlockSpec((tm, width), lambda i, *_: (i, 0))],
            out_specs=pl.BlockSpec(memory_space=pl.ANY),
            scratch_shapes=[pltpu.VMEM((FFN_TM, width), h2p.dtype),
                            pltpu.SemaphoreType.DMA(()), pltpu.SemaphoreType.DMA(())],
        ),
        compiler_params=_cparams(("arbitrary",)),
        name="moe_dispatch",
    )(seg_fill, seg_end, _step_slots(pos, tm), h2p)


def _ffn_kernel(te_ref, nv_ref, x_ref, wg_ref, wu_ref, wd_ref, o_ref, wg_bf, wu_bf, wd_bf):
    i = pl.program_id(0)
    nvalid = nv_ref[i]

    @pl.when((i == 0) | (te_ref[i] != te_ref[jnp.maximum(i - 1, 0)]))
    def _():
        wg_bf[...] = wg_ref[...].astype(BF16)
        wu_bf[...] = wu_ref[...].astype(BF16)
        wd_bf[...] = wd_ref[...].astype(BF16)

    @pl.when(nvalid > 0)
    def _():
        x = _unpack_bf16_pairs(x_ref[...])
        hg = _dot(x, wg_bf[...])
        hu = _dot(x, wu_bf[...])
        h = (hg * _sigmoid(hg) * hu).astype(BF16)
        o_ref[...] = _dot(h, wd_bf[...])

    @pl.when(nvalid == 0)
    def _():
        o_ref[...] = jnp.zeros_like(o_ref)


def _expert_ffn(x_sorted, tile_expert, tile_valid, wg, wu, wd, layer):
    n_slots = x_sorted.shape[0]
    tm = FFN_TM
    return pl.pallas_call(
        _ffn_kernel,
        out_shape=jax.ShapeDtypeStruct((n_slots, D_MODEL), F32),
        grid_spec=pltpu.PrefetchScalarGridSpec(
            num_scalar_prefetch=2,
            grid=(n_slots // tm,),
            in_specs=[pl.BlockSpec((tm, D_MODEL // 2), lambda i, te, nv: (i, 0)),
                      pl.BlockSpec((None, None, D_MODEL, EXPERT_FF), lambda i, te, nv: (layer, te[i], 0, 0)),
                      pl.BlockSpec((None, None, D_MODEL, EXPERT_FF), lambda i, te, nv: (layer, te[i], 0, 0)),
                      pl.BlockSpec((None, None, EXPERT_FF, D_MODEL), lambda i, te, nv: (layer, te[i], 0, 0))],
            out_specs=pl.BlockSpec((tm, D_MODEL), lambda i, te, nv: (i, 0)),
            scratch_shapes=[pltpu.VMEM((D_MODEL, EXPERT_FF), BF16), pltpu.VMEM((D_MODEL, EXPERT_FF), BF16),
                            pltpu.VMEM((EXPERT_FF, D_MODEL), BF16)],
        ),
        compiler_params=_cparams(("arbitrary",)),
        name="moe_ffn",
    )(tile_expert, tile_valid, x_sorted, wg, wu, wd)


def _combine_kernel(pos_ref, posn_ref, wgt_ref, x1_ref, h_ref, gt_ref,
                    wsg_ref, wsu_ref, wsd_ref, y_hbm, o_ref, ybuf, sems):
    i = pl.program_id(0)
    nsteps = pl.num_programs(0)
    tm = x1_ref.shape[0]

    def issue(slot, pr):
        def body(t, carry):
            for k in range(TOP_K):
                pltpu.make_async_copy(y_hbm.at[pl.ds(pr[0, 0, t * TOP_K + k], 1)],
                                      ybuf.at[slot, k, pl.ds(t, 1)], sems.at[slot]).start()
            return carry
        lax.fori_loop(0, tm, body, 0)

    slot = i % 2

    @pl.when(i == 0)
    def _():
        issue(0, pos_ref)

    for nxt in range(2):
        @pl.when((i + 1 < nsteps) & (slot != nxt))
        def _(nxt=nxt):
            issue(nxt, posn_ref)

    hb = h_ref[...].astype(BF16)
    hg = _dot(hb, wsg_ref[...])
    hu = _dot(hb, wsu_ref[...])
    shared = _dot((hg * _sigmoid(hg) * hu).astype(BF16), wsd_ref[...])

    for k in range(TOP_K):
        pltpu.make_async_copy(y_hbm.at[pl.ds(0, tm)], ybuf.at[slot, k], sems.at[slot]).wait()

    acc = shared
    for k in range(TOP_K):
        acc = acc + wgt_ref[:, k:k + 1] * ybuf[slot, k]
    o_ref[...] = x1_ref[...] + gt_ref[...] * acc


def _combine(y_sorted, pos, wgt_t, x1, h2, mod5, layer, wsg, wsu, wsd, geo):
    n = x1.shape[0]
    tm = COMBINE_TM
    nsteps = n // tm
    smem_cur = pl.BlockSpec((1, 1, TOP_K * tm), lambda i: (i, 0, 0), memory_space=pltpu.SMEM)
    smem_next = pl.BlockSpec((1, 1, TOP_K * tm), lambda i: (jnp.minimum(i + 1, nsteps - 1), 0, 0),
                             memory_space=pltpu.SMEM)
    slots = _step_slots(pos, tm)

    def gt_map(i):
        b, _ = geo.locate(i * tm)
        return (layer, 5, b, 0, 0)

    row = pl.BlockSpec((tm, D_MODEL), lambda i: (i, 0))
    whole = lambda a, b: pl.BlockSpec((a, b), lambda i: (0, 0))
    return pl.pallas_call(
        _combine_kernel,
        out_shape=jax.ShapeDtypeStruct((n, D_MODEL), F32),
        grid=(nsteps,),
        in_specs=[smem_cur, smem_next,
                  pl.BlockSpec((tm, TOP_K), lambda i: (i, 0)),
                  row, row,
                  pl.BlockSpec((None, None, None, 1, D_MODEL), gt_map),
                  whole(D_MODEL, EXPERT_FF), whole(D_MODEL, EXPERT_FF), whole(EXPERT_FF, D_MODEL),
                  pl.BlockSpec(memory_space=pl.ANY)],
        out_specs=row,
        scratch_shapes=[pltpu.VMEM((2, TOP_K, tm, D_MODEL), F32), pltpu.SemaphoreType.DMA((2,))],
        compiler_params=_cparams(("arbitrary",)),
        name="moe_combine",
    )(slots, slots, wgt_t, x1, h2, mod5, wsg, wsu, wsd, y_sorted)


def _rope_tables(t_max):
    pos = jnp.arange(t_max, dtype=jnp.int32)

    def angles(p, n_pairs):
        inv = ROPE_BASE ** (-jnp.arange(n_pairs, dtype=F32) / n_pairs)
        ang = p.astype(F32)[:, None] * inv[None, :]
        return jnp.cos(ang), jnp.sin(ang)

    c, s = angles(pos, HEAD_DIM // 2)
    cr, sr = angles(pos // GRID_W, HEAD_DIM // 4)
    cc, sc = angles(pos % GRID_W, HEAD_DIM // 4)
    return (jnp.concatenate([c, c], -1), jnp.concatenate([-s, s], -1),
            jnp.concatenate([cr, cr, cc, cc], -1), jnp.concatenate([-sr, sr, -sc, sc], -1))


def _moe_plan(counts, n_slots):
    padded = (counts + FFN_TM - 1) // FFN_TM * FFN_TM
    seg_end = jnp.cumsum(padded)
    seg_start = seg_end - padded
    tile_row0 = jnp.arange(n_slots // FFN_TM, dtype=I32) * FFN_TM
    tile_expert = jnp.minimum(jnp.sum(tile_row0[:, None] >= seg_end[None, :], axis=1), N_EXPERTS - 1).astype(I32)
    tile_valid = jnp.clip(seg_start[tile_expert] + counts[tile_expert] - tile_row0, 0, FFN_TM).astype(I32)
    return (seg_start.astype(I32), (seg_start + counts).astype(I32), seg_end.astype(I32),
            tile_expert, tile_valid)


def _layer(x, mod5, layer, p, tabs, geo):
    n = x.shape[0]
    proj = _inproj(x, mod5, layer, p["g1"], p["w_in"], tabs, p["gains"], geo)
    y_ret = _retention(proj, p["lg_f"], p["lg_b"], p["ret_gn"], geo)
    y_na = _neighbourhood(proj, p["na_tab"], geo)
    y_ga = jnp.concatenate([_flash_group(proj, 0, geo.bp, geo.tp),
                            _flash_group(proj, geo.n_p, geo.bs, geo.ts)], axis=0)
    x1, h2, h2p = _outproj(y_ret, y_na, y_ga, x, p["w_out"], p["na_on"], p["ga_on"], mod5, layer, p["g2"], geo)
    idx, wgt, rank, cnt = _router(h2, p["w_router_t"], p["router_bias"])
    n_slots = n * TOP_K + N_EXPERTS * FFN_TM
    seg_start, seg_fill, seg_end, tile_expert, tile_valid = _moe_plan(cnt[:, 0].astype(I32), n_slots)
    pos = rank
    for e in range(N_EXPERTS):
        pos = pos + jnp.where(idx == e, seg_start[e], 0)
    x_sorted = _dispatch(h2p, pos, seg_fill, seg_end, n_slots)
    y_sorted = _expert_ffn(x_sorted, tile_expert, tile_valid, p["w_eg"], p["w_eu"], p["w_ed"], layer)
    return _combine(y_sorted, pos, wgt.T, x1, h2, mod5, layer, p["w_sg"], p["w_su"], p["w_sd"], geo)


def kernel(x_prompt, x_sample, c_prompt, c_sample, w_ada, b_ada, norm1, w_in, ret_decay_fwd, ret_decay_bwd, ret_norm, na_q_norm, na_k_norm, na_rpb, na_out_norm, ga_q_norm, ga_k_norm, ga_out_norm, w_out, norm2, w_router, router_bias, w_exp_gate, w_exp_up, w_exp_down, w_sh_gate, w_sh_up, w_sh_down):
    bp, tp, d = x_prompt.shape
    bs, ts, _ = x_sample.shape
    depth = w_ada.shape[0]
    geo = Geom(bp, tp, bs, ts)
    assert d == D_MODEL and w_in.shape[-1] == PROJ_W
    for t in (tp, ts):
        assert t % max(PROJ_TM, FLASH_TQ, NA_SLAB_ROWS * GRID_W) == 0 and t % min(FLASH_TK, t) == 0

    x = jnp.concatenate([x_prompt.reshape(bp * tp, d), x_sample.reshape(bs * ts, d)], axis=0)
    c_all = jnp.concatenate([c_prompt, c_sample], axis=0)
    c_pad = jnp.zeros((8 * pl.cdiv(geo.nb, 8), d), F32).at[:geo.nb].set(c_all)
    mod = _adaln(c_pad, w_ada, b_ada)
    rows = mod.shape[1]
    mod5 = mod.reshape(depth, rows, N_MOD, 1, d).transpose(0, 2, 1, 3, 4)
    tabs = _rope_tables(max(tp, ts))

    for l in range(depth):
        p = {
            "g1": norm1[l].reshape(1, d),
            "w_in": w_in[l].astype(BF16),
            "gains": jnp.stack([na_q_norm[l], na_k_norm[l], ga_q_norm[l], ga_k_norm[l]]),
            "lg_f": jax.nn.log_sigmoid(ret_decay_fwd[l].astype(F32)),
            "lg_b": jax.nn.log_sigmoid(ret_decay_bwd[l].astype(F32)),
            "ret_gn": ret_norm[l].reshape(1, RET_W),
            "na_tab": _na_bias_table(na_rpb[l]),
            "na_on": na_out_norm[l].reshape(1, NA_W),
            "ga_on": ga_out_norm[l].reshape(1, GA_W),
            "w_out": w_out[l].astype(BF16),
            "g2": norm2[l].reshape(1, d),
            "w_router_t": w_router[l].astype(F32).T,
            "router_bias": jnp.broadcast_to(router_bias[l].astype(F32)[:, None], (N_EXPERTS, HEAD_DIM)),
            "w_eg": w_exp_gate,
            "w_eu": w_exp_up,
            "w_ed": w_exp_down,
            "w_sg": w_sh_gate[l].astype(BF16),
            "w_su": w_sh_up[l].astype(BF16),
            "w_sd": w_sh_down[l].astype(BF16),
        }
        x = _layer(x, mod5, l, p, tabs, geo)

    y_prompt = x[:geo.n_p].reshape(bp, tp, d)
    y_sample = x[geo.n_p:].reshape(bs, ts, d)
    return (y_prompt, y_sample)
```

```python
import dataclasses
import functools

import numpy as np
import jax
import jax.numpy as jnp
from jax import lax
from jax.experimental import pallas as pl
from jax.experimental.pallas import tpu as pltpu

F32 = jnp.float32
BF16 = jnp.bfloat16
I32 = jnp.int32

D_MODEL = 2048
HEAD_DIM = 128
RET_W = 512
NA_W = 512
GA_W = 1024
GA_KV_W = 256
PROJ_W = 4 * RET_W + 3 * NA_W + GA_W + 2 * GA_KV_W
RET_CHUNK = 128
GRID_W = 64
NA_WIN_R = 8
NA_WIN_C = 16
ROPE_BASE = 10000.0
N_EXPERTS = 64
N_GROUPS = 8
GROUP_SIZE = N_EXPERTS // N_GROUPS
TOPK_GROUPS = 4
TOP_K = 8
EXPERT_FF = 512
ROUTED_SCALE = 2.5
N_MOD = 6
EPS = 1e-6
QK_SCALE = HEAD_DIM ** -0.5
LOG2E = 1.4426950408889634
NEG_INF = float("-inf")
MASK_VALUE = -1e30

VMEM_LIMIT_BYTES = 56 * 1024 * 1024

PROJ_TN = 512
PROJ_TM = 1024
NA_QROWS = 8
NA_SLAB_ROWS = 16
NA_KBLK_ROWS = 4
NA_DOT_ROWS = 256
FLASH_TQ = 1024
FLASH_TK = 4096
FLASH_DOT_ROWS = 256
OUT_TM = 256
ROUTER_TM = 512
DISPATCH_TM = 512
FFN_TM = 512
COMBINE_TM = 128
ADALN_TN = 1024


@dataclasses.dataclass(frozen=True)
class Geom:
    bp: int
    tp: int
    bs: int
    ts: int

    @property
    def n_p(self):
        return self.bp * self.tp

    @property
    def n(self):
        return self.bp * self.tp + self.bs * self.ts

    @property
    def nb(self):
        return self.bp + self.bs

    def locate(self, row):
        in_p = row < self.n_p
        rs = jnp.maximum(row - self.n_p, 0)
        b = jnp.where(in_p, row // self.tp, self.bp + rs // self.ts)
        pos = jnp.where(in_p, row % self.tp, rs % self.ts)
        return b, pos

    def seq_len(self, row):
        return jnp.where(row < self.n_p, self.tp, self.ts)


def _cparams(sem, vmem=VMEM_LIMIT_BYTES):
    return pltpu.CompilerParams(dimension_semantics=sem, vmem_limit_bytes=vmem)


def _dot(a, b):
    return jnp.dot(a, b, preferred_element_type=F32)


def _dot_nt(a, b):
    return lax.dot_general(a, b, (((1,), (1,)), ((), ())), preferred_element_type=F32)


def _sigmoid(x):
    return 1.0 / (1.0 + jnp.exp(-x))


def _rms(x, g):
    return x * lax.rsqrt(jnp.mean(x * x, axis=-1, keepdims=True) + EPS) * g


def _adaln_kernel(c_ref, w_ref, b_ref, o_ref):
    c = c_ref[...]
    s = c * _sigmoid(c)
    o_ref[...] = jnp.dot(s, w_ref[...], preferred_element_type=F32,
                         precision=lax.Precision.HIGHEST) + b_ref[...]


def _adaln(c_pad, w_ada, b_ada):
    depth, d, w = w_ada.shape
    rows = c_pad.shape[0]
    return pl.pallas_call(
        _adaln_kernel,
        out_shape=jax.ShapeDtypeStruct((depth, rows, w), F32),
        grid=(depth, w // ADALN_TN),
        in_specs=[
            pl.BlockSpec((rows, d), lambda l, j: (0, 0)),
            pl.BlockSpec((None, d, ADALN_TN), lambda l, j: (l, 0, j)),
            pl.BlockSpec((None, 1, ADALN_TN), lambda l, j: (l, 0, j)),
        ],
        out_specs=pl.BlockSpec((None, rows, ADALN_TN), lambda l, j: (l, 0, j)),
        compiler_params=_cparams(("parallel", "parallel")),
        name="adaln",
    )(c_pad, w_ada, b_ada.reshape(depth, 1, w))


def _rope_ret(x, cos, sin):
    return x * cos + pltpu.roll(x, HEAD_DIM // 2, 1) * sin


def _rope_axial(x, cos, sin):
    lane = lax.broadcasted_iota(I32, x.shape, 1)
    first = (lane & (HEAD_DIM // 4)) == 0
    rot = jnp.where(first, pltpu.roll(x, HEAD_DIM - HEAD_DIM // 4, 1), pltpu.roll(x, HEAD_DIM // 4, 1))
    return x * cos + rot * sin


def _inproj_kernel(x_ref, sc_ref, sh_ref, g1_ref, w_ref, rc_ref, rs_ref, ac_ref, as_ref, gains_ref,
                   o_ref, h_scr, acc_scr):
    j = pl.program_id(1)

    @pl.when(j == 0)
    def _():
        h = _rms(x_ref[...], g1_ref[...]) * (1.0 + sc_ref[...]) + sh_ref[...]
        h_scr[...] = h.astype(BF16)

    acc_scr[...] = _dot(h_scr[...], w_ref[...])
    heads = PROJ_TN // HEAD_DIM

    def head(hh):
        return acc_scr[:, hh * HEAD_DIM:(hh + 1) * HEAD_DIM]

    def put(hh, val):
        o_ref[:, hh * HEAD_DIM:(hh + 1) * HEAD_DIM] = val.astype(BF16)

    @pl.when(j == 0)
    def _():
        for hh in range(heads):
            put(hh, _rope_ret(head(hh), rc_ref[...], rs_ref[...]))

    @pl.when(j == 1)
    def _():
        for hh in range(heads):
            put(hh, _rope_ret(head(hh), rc_ref[...], rs_ref[...]) * QK_SCALE)

    @pl.when((j == 2) | (j == 6))
    def _():
        o_ref[...] = acc_scr[...].astype(BF16)

    @pl.when(j == 3)
    def _():
        a = acc_scr[...]
        o_ref[...] = (a * _sigmoid(a)).astype(BF16)

    @pl.when(j == 4)
    def _():
        for hh in range(heads):
            put(hh, _rms(head(hh), gains_ref[0:1, :]))

    @pl.when(j == 5)
    def _():
        for hh in range(heads):
            put(hh, _rms(head(hh), gains_ref[1:2, :]))

    @pl.when((j == 7) | (j == 8))
    def _():
        for hh in range(heads):
            put(hh, _rope_axial(_rms(head(hh), gains_ref[2:3, :]), ac_ref[...], as_ref[...])
                * (QK_SCALE * LOG2E))

    @pl.when(j == 9)
    def _():
        for hh in range(2):
            put(hh, _rope_axial(_rms(head(hh), gains_ref[3:4, :]), ac_ref[...], as_ref[...]))
        o_ref[:, 2 * HEAD_DIM:] = acc_scr[:, 2 * HEAD_DIM:].astype(BF16)


def _inproj(x, mod5, layer, g1, w_in_bf, tabs, gains, geo):
    n = x.shape[0]
    tm = PROJ_TM

    def mod_map(chunk):
        def f(i, j):
            b, _ = geo.locate(i * tm)
            return (layer, chunk, b, 0, 0)
        return f

    def tab_map(i, j):
        _, pos = geo.locate(i * tm)
        return (pos // tm, 0)

    mod_spec = lambda chunk: pl.BlockSpec((None, None, None, 1, D_MODEL), mod_map(chunk))
    tab_spec = pl.BlockSpec((tm, HEAD_DIM), tab_map)
    return pl.pallas_call(
        _inproj_kernel,
        out_shape=jax.ShapeDtypeStruct((n, PROJ_W), BF16),
        grid=(n // tm, PROJ_W // PROJ_TN),
        in_specs=[
            pl.BlockSpec((tm, D_MODEL), lambda i, j: (i, 0)),
            mod_spec(1), mod_spec(0),
            pl.BlockSpec((1, D_MODEL), lambda i, j: (0, 0)),
            pl.BlockSpec((D_MODEL, PROJ_TN), lambda i, j: (0, j)),
            tab_spec, tab_spec, tab_spec, tab_spec,
            pl.BlockSpec((4, HEAD_DIM), lambda i, j: (0, 0)),
        ],
        out_specs=pl.BlockSpec((tm, PROJ_TN), lambda i, j: (i, j)),
        scratch_shapes=[pltpu.VMEM((tm, D_MODEL), BF16), pltpu.VMEM((tm, PROJ_TN), F32)],
        compiler_params=_cparams(("parallel", "arbitrary")),
        name="inproj",
    )(x, mod5, mod5, g1, w_in_bf, tabs[0], tabs[1], tabs[2], tabs[3], gains)


def _ret_kernel(lg_ref, q_ref, k_ref, v_ref, *rest, reverse, geo):
    if reverse:
        o_ref, s_scr, tab_scr = rest
    else:
        sg_ref, ob_ref, gn_ref, o_ref, s_scr, tab_scr = rest
    i = pl.program_id(0)
    nchunks = pl.num_programs(0)
    c = RET_CHUNK
    heads = RET_W // HEAD_DIM

    @pl.when(i == 0)
    def _():
        row = lax.broadcasted_iota(I32, (c, c), 0).astype(F32)
        col = lax.broadcasted_iota(I32, (c, c), 1).astype(F32)
        for hh in range(heads):
            lg = lg_ref[hh]
            if reverse:
                diff = col - row
                decay = jnp.where(diff > 0, jnp.exp(lg * jnp.maximum(diff, 0.0)), 0.0)
                qdec = jnp.exp(lg * (c - row))
                kdec = jnp.exp(lg * row)
            else:
                diff = row - col
                decay = jnp.where(diff >= 0, jnp.exp(lg * jnp.maximum(diff, 0.0)), 0.0)
                qdec = jnp.exp(lg * (row + 1.0))
                kdec = jnp.exp(lg * (c - 1.0 - row))
            tab_scr[hh, 0] = decay
            tab_scr[hh, 1] = qdec
            tab_scr[hh, 2] = kdec
            tab_scr[hh, 3] = jnp.exp(jnp.zeros((c, c), F32) + lg * c)

    chunk = (nchunks - 1 - i) if reverse else i
    row0 = chunk * c
    _, pos = geo.locate(row0)
    boundary = (pos + c == geo.seq_len(row0)) if reverse else (pos == 0)

    @pl.when(boundary)
    def _():
        s_scr[...] = jnp.zeros_like(s_scr)

    for hh in range(heads):
        sl = slice(hh * HEAD_DIM, (hh + 1) * HEAD_DIM)
        qh = q_ref[:, sl]
        kh = k_ref[:, sl]
        vh = v_ref[:, sl]
        s = _dot_nt(qh, kh) * tab_scr[hh, 0]
        o = _dot(s.astype(BF16), vh)
        qd = (qh.astype(F32) * tab_scr[hh, 1]).astype(BF16)
        o = o + _dot(qd, s_scr[hh].astype(BF16))
        kd_t = (kh.astype(F32) * tab_scr[hh, 2]).T.astype(BF16)
        s_scr[hh] = s_scr[hh] * tab_scr[hh, 3] + _dot(kd_t, vh)
        if reverse:
            o_ref[:, sl] = o
        else:
            tot = o + ob_ref[:, sl]
            y = _rms(tot, gn_ref[:, sl])
            o_ref[:, sl] = (sg_ref[:, sl].astype(F32) * y).astype(BF16)


def _retention(proj, lg_f, lg_b, ret_gn, geo):
    n = proj.shape[0]
    nchunks = n // RET_CHUNK
    c = RET_CHUNK
    heads = RET_W // HEAD_DIM
    scratch = [pltpu.VMEM((heads, HEAD_DIM, HEAD_DIM), F32), pltpu.VMEM((heads, 4, c, c), F32)]
    smem = pl.BlockSpec(memory_space=pltpu.SMEM)

    def col_spec(colblk, rev):
        if rev:
            return pl.BlockSpec((c, RET_W), lambda i: (nchunks - 1 - i, colblk))
        return pl.BlockSpec((c, RET_W), lambda i: (i, colblk))

    o_bwd = pl.pallas_call(
        functools.partial(_ret_kernel, reverse=True, geo=geo),
        out_shape=jax.ShapeDtypeStruct((n, RET_W), F32),
        grid=(nchunks,),
        in_specs=[smem, col_spec(0, True), col_spec(1, True), col_spec(2, True)],
        out_specs=pl.BlockSpec((c, RET_W), lambda i: (nchunks - 1 - i, 0)),
        scratch_shapes=scratch,
        compiler_params=_cparams(("arbitrary",)),
        name="ret_bwd",
    )(lg_b, proj, proj, proj)
    return pl.pallas_call(
        functools.partial(_ret_kernel, reverse=False, geo=geo),
        out_shape=jax.ShapeDtypeStruct((n, RET_W), BF16),
        grid=(nchunks,),
        in_specs=[smem, col_spec(0, False), col_spec(1, False), col_spec(2, False), col_spec(3, False),
                  pl.BlockSpec((c, RET_W), lambda i: (i, 0)),
                  pl.BlockSpec((1, RET_W), lambda i: (0, 0))],
        out_specs=pl.BlockSpec((c, RET_W), lambda i: (i, 0)),
        scratch_shapes=scratch,
        compiler_params=_cparams(("arbitrary",)),
        name="ret_fwd",
    )(lg_f, proj, proj, proj, proj, o_bwd, ret_gn)


def _na_geometry(i, geo):
    rp, rs = geo.tp // GRID_W, geo.ts // GRID_W
    r0g = i * NA_QROWS
    in_p = r0g < geo.bp * rp
    rsmp = jnp.maximum(r0g - geo.bp * rp, 0)
    rows = jnp.where(in_p, rp, rs)
    lr0 = jnp.where(in_p, r0g % rp, rsmp % rs)
    slab = jnp.clip(lr0 - NA_WIN_R // 2, 0, rows - NA_SLAB_ROWS)
    return rows, lr0, slab, r0g - lr0


def _na_kernel(q_ref, k0, k1, k2, k3, v0, v1, v2, v3, tab_ref, o_ref):
    k = jnp.concatenate([k0[...], k1[...], k2[...], k3[...]], axis=0)
    v = jnp.concatenate([v0[...], v1[...], v2[...], v3[...]], axis=0)
    s = _chunked(lambda x: _dot_nt(x, k), q_ref[...], NA_DOT_ROWS) * QK_SCALE + tab_ref[...]
    e = jnp.exp(s - jnp.max(s, axis=-1, keepdims=True))
    prob = (e / jnp.sum(e, axis=-1, keepdims=True)).astype(BF16)
    o_ref[...] = _chunked(lambda x: _dot(x, v), prob, NA_DOT_ROWS).astype(BF16)


def _na_bias_table(rpb):
    rpb = rpb.astype(F32)
    rows = jnp.stack([rpb[:, NA_WIN_R - 1 - v:2 * NA_WIN_R - 1 - v, :] for v in range(NA_WIN_R)], axis=1)
    width = 2 * GRID_W
    lead = GRID_W - NA_WIN_C
    p = jnp.pad(rows, ((0, 0), (0, 0), (0, 0), (lead, width - lead - (2 * NA_WIN_C - 1))))
    skew = jnp.tile(p, GRID_W)[..., :GRID_W * (width - 1)].reshape(p.shape[:-1] + (GRID_W, width - 1))
    toe = skew[..., GRID_W - 1:]
    c = np.arange(GRID_W)[:, None]
    j = np.arange(GRID_W)[None, :]
    cs = np.clip(c - NA_WIN_C // 2, 0, GRID_W - NA_WIN_C)
    valid = (j >= cs) & (j < cs + NA_WIN_C)
    tab8 = jnp.where(valid[None, None, :, None, :], toe.transpose(0, 1, 3, 2, 4), MASK_VALUE)
    blocks = []
    for d in range(NA_SLAB_ROWS // NA_WIN_R + 1):
        per_row = []
        for p in range(NA_QROWS):
            first = (max(p - NA_WIN_R // 2, 0), p, min(p + NA_WIN_R // 2, NA_WIN_R))[d]
            win = tab8[:, d * (NA_WIN_R // 2) + p - first]
            per_row.append(jnp.pad(win, ((0, 0), (0, 0), (first, NA_SLAB_ROWS - NA_WIN_R - first), (0, 0)),
                                   constant_values=MASK_VALUE))
        blocks.append(jnp.stack(per_row, axis=1))
    return jnp.stack(blocks, axis=1).reshape(rpb.shape[0], len(blocks), NA_QROWS * GRID_W,
                                             NA_SLAB_ROWS * GRID_W)


def _neighbourhood(proj, bias_tab, geo):
    n = proj.shape[0]
    tq = NA_QROWS * GRID_W
    blk = NA_KBLK_ROWS * GRID_W
    nblk = NA_SLAB_ROWS // NA_KBLK_ROWS
    heads = NA_W // HEAD_DIM
    qcol, kcol, vcol = 4 * heads, 5 * heads, 6 * heads

    def kv_spec(col0, m):
        def f(h, i):
            _, _, slab, seq_row0 = _na_geometry(i, geo)
            return ((seq_row0 + slab) // NA_KBLK_ROWS + m, col0 + h)
        return pl.BlockSpec((blk, HEAD_DIM), f)

    def tab_map(h, i):
        _, lr0, slab, _ = _na_geometry(i, geo)
        return (h, (lr0 - slab) // (NA_WIN_R // 2), 0, 0)

    return pl.pallas_call(
        _na_kernel,
        out_shape=jax.ShapeDtypeStruct((n, NA_W), BF16),
        grid=(heads, n // tq),
        in_specs=[pl.BlockSpec((tq, HEAD_DIM), lambda h, i: (i, qcol + h))]
        + [kv_spec(kcol, m) for m in range(nblk)]
        + [kv_spec(vcol, m) for m in range(nblk)]
        + [pl.BlockSpec((None, None, tq, NA_SLAB_ROWS * GRID_W), tab_map)],
        out_specs=pl.BlockSpec((tq, HEAD_DIM), lambda h, i: (i, h)),
        compiler_params=_cparams(("parallel", "parallel")),
        name="natten",
    )(*([proj] * (1 + 2 * nblk)), bias_tab)


def _chunked(fn, x, rows):
    return jnp.concatenate([fn(x[c * rows:(c + 1) * rows]) for c in range(x.shape[0] // rows)], axis=0)


def _flash_kernel(q_ref, k_ref, v_ref, o_ref, m_scr, acc_scr, va_scr):
    ki = pl.program_id(3)
    groups = q_ref.shape[1] // HEAD_DIM
    tk = k_ref.shape[0]

    @pl.when(ki == 0)
    def _():
        m_scr[...] = jnp.full_like(m_scr, NEG_INF)
        acc_scr[...] = jnp.zeros_like(acc_scr)
        lane = lax.broadcasted_iota(I32, (tk, HEAD_DIM), 1)
        va_scr[:, HEAD_DIM:] = jnp.where(lane == 0, 1.0, 0.0).astype(BF16)

    va_scr[:, :HEAD_DIM] = v_ref[...]
    k = k_ref[...]
    va = va_scr[...]
    for gq in range(groups):
        q = q_ref[:, gq * HEAD_DIM:(gq + 1) * HEAD_DIM]
        s = _chunked(lambda x: _dot_nt(x, k), q, FLASH_DOT_ROWS)
        m_prev = m_scr[gq][:, :1]
        m_next = jnp.maximum(m_prev, jnp.max(s, axis=1, keepdims=True))
        alpha = jnp.exp2(m_prev - m_next)
        p = jnp.exp2(s - m_next).astype(BF16)
        acc_scr[gq] = alpha * acc_scr[gq] + _chunked(lambda x: _dot(x, va), p, FLASH_DOT_ROWS)
        m_scr[gq] = jnp.broadcast_to(m_next, m_scr.shape[1:])

    @pl.when(ki == pl.num_programs(3) - 1)
    def _():
        for gq in range(groups):
            acc = acc_scr[gq]
            o_ref[:, gq * HEAD_DIM:(gq + 1) * HEAD_DIM] = (
                acc[:, :HEAD_DIM] / acc[:, HEAD_DIM:HEAD_DIM + 1]).astype(BF16)


def _flash_group(proj, row0, batch, t):
    tq, tk = min(FLASH_TQ, t), min(FLASH_TK, t)
    assert row0 % tq == 0 and row0 % tk == 0 and t % tq == 0 and t % tk == 0
    kv_heads = GA_KV_W // HEAD_DIM
    qw = GA_W // kv_heads
    groups = qw // HEAD_DIM
    qcol0 = (4 * RET_W + 3 * NA_W) // qw
    kcol0 = (4 * RET_W + 3 * NA_W + GA_W) // HEAD_DIM
    vcol0 = kcol0 + kv_heads
    return pl.pallas_call(
        _flash_kernel,
        out_shape=jax.ShapeDtypeStruct((batch * t, GA_W), BF16),
        grid=(batch, kv_heads, t // tq, t // tk),
        in_specs=[
            pl.BlockSpec((tq, qw), lambda b, h, qi, ki: ((row0 + b * t) // tq + qi, qcol0 + h)),
            pl.BlockSpec((tk, HEAD_DIM), lambda b, h, qi, ki: ((row0 + b * t) // tk + ki, kcol0 + h)),
            pl.BlockSpec((tk, HEAD_DIM), lambda b, h, qi, ki: ((row0 + b * t) // tk + ki, vcol0 + h)),
        ],
        out_specs=pl.BlockSpec((tq, qw), lambda b, h, qi, ki: ((b * t) // tq + qi, h)),
        scratch_shapes=[pltpu.VMEM((groups, tq, HEAD_DIM), F32),
                        pltpu.VMEM((groups, tq, 2 * HEAD_DIM), F32),
                        pltpu.VMEM((tk, 2 * HEAD_DIM), BF16)],
        compiler_params=_cparams(("parallel", "parallel", "parallel", "arbitrary")),
        name="flash_gqa",
    )(proj, proj, proj)


def _outproj_kernel(yr_ref, yn_ref, yg_ref, x_ref, w_ref, non_ref, gon_ref, gt_ref, sc_ref, sh_ref, g2_ref,
                    x1_ref, h2_ref, h2p_ref):
    ynn = _rms(yn_ref[...].astype(F32), non_ref[...]).astype(BF16)
    ygn = _rms(yg_ref[...].astype(F32), gon_ref[...]).astype(BF16)
    acc = _dot(yr_ref[...], w_ref[0:RET_W, :])
    acc = acc + _dot(ynn, w_ref[RET_W:RET_W + NA_W, :])
    acc = acc + _dot(ygn, w_ref[RET_W + NA_W:, :])
    x1 = x_ref[...] + gt_ref[...] * acc
    x1_ref[...] = x1
    h2 = _rms(x1, g2_ref[...]) * (1.0 + sc_ref[...]) + sh_ref[...]
    h2_ref[...] = h2
    h2p_ref[...] = _pack_bf16_pairs(h2)


def _pack_bf16_pairs(x):
    w = x.shape[1] // 2
    bits = pltpu.bitcast(x.astype(BF16).astype(F32), jnp.uint32)
    return (bits[:, w:] & jnp.uint32(0xFFFF0000)) | (bits[:, :w] >> 16)


def _unpack_bf16_pairs(p):
    lo = pltpu.bitcast(p << 16, F32)
    hi = pltpu.bitcast(p & jnp.uint32(0xFFFF0000), F32)
    return jnp.concatenate([lo, hi], axis=1).astype(BF16)


def _outproj(y_ret, y_na, y_ga, x, w_out_bf, na_on, ga_on, mod5, layer, g2, geo):
    n = x.shape[0]
    tm = OUT_TM

    def mod_spec(chunk):
        def f(i):
            b, _ = geo.locate(i * tm)
            return (layer, chunk, b, 0, 0)
        return pl.BlockSpec((None, None, None, 1, D_MODEL), f)

    row = lambda w: pl.BlockSpec((tm, w), lambda i: (i, 0))
    return pl.pallas_call(
        _outproj_kernel,
        out_shape=(jax.ShapeDtypeStruct((n, D_MODEL), F32), jax.ShapeDtypeStruct((n, D_MODEL), F32),
                   jax.ShapeDtypeStruct((n, D_MODEL // 2), jnp.uint32)),
        grid=(n // tm,),
        in_specs=[row(RET_W), row(NA_W), row(GA_W), row(D_MODEL),
                  pl.BlockSpec((D_MODEL, D_MODEL), lambda i: (0, 0)),
                  pl.BlockSpec((1, NA_W), lambda i: (0, 0)),
                  pl.BlockSpec((1, GA_W), lambda i: (0, 0)),
                  mod_spec(2), mod_spec(4), mod_spec(3),
                  pl.BlockSpec((1, D_MODEL), lambda i: (0, 0))],
        out_specs=(row(D_MODEL), row(D_MODEL), row(D_MODEL // 2)),
        compiler_params=_cparams(("parallel",)),
        name="outproj",
    )(y_ret, y_na, y_ga, x, w_out_bf, na_on, ga_on, mod5, mod5, mod5, g2)


def _first_index_of_max(vals, ids, axes, sentinel):
    m = vals
    for ax in axes:
        m = jnp.max(m, axis=ax, keepdims=True)
    cand = jnp.where(vals == m, ids, sentinel)
    for ax in axes:
        cand = jnp.min(cand, axis=ax, keepdims=True)
    return m, cand


def _router_kernel(h_ref, w_ref, b_ref, idx_ref, wgt_ref, rank_ref, cnt_ref, cnt_scr, tri_scr):
    i = pl.program_id(0)
    tm = h_ref.shape[0]

    @pl.when(i == 0)
    def _():
        cnt_scr[...] = jnp.zeros_like(cnt_scr)
        r = lax.broadcasted_iota(I32, (tm, tm), 0)
        c = lax.broadcasted_iota(I32, (tm, tm), 1)
        tri_scr[...] = jnp.where(r < c, 1.0, 0.0).astype(BF16)

    logits = lax.dot_general(w_ref[...], h_ref[...], (((1,), (1,)), ((), ())),
                             preferred_element_type=F32, precision=lax.Precision.HIGHEST)
    scores = _sigmoid(logits)
    choice = scores + b_ref[...][:, :1]
    shape3 = (N_GROUPS, GROUP_SIZE, tm)
    choice3 = choice.reshape(shape3)
    scores3 = scores.reshape(shape3)
    sub = lax.broadcasted_iota(I32, shape3, 1)
    grp = lax.broadcasted_iota(I32, shape3, 0)
    eid = grp * GROUP_SIZE + sub

    m1, i1 = _first_index_of_max(choice3, sub, (1,), GROUP_SIZE)
    rest = jnp.where(sub == i1, NEG_INF, choice3)
    m2 = jnp.max(rest, axis=1, keepdims=True)
    gscore = m1 + m2

    gid = lax.broadcasted_iota(I32, gscore.shape, 0)
    gsel = jnp.zeros(gscore.shape, F32)
    for _ in range(TOPK_GROUPS):
        _, gi = _first_index_of_max(gscore, gid, (0,), N_GROUPS)
        hit = gid == gi
        gsel = jnp.where(hit, 1.0, gsel)
        gscore = jnp.where(hit, NEG_INF, gscore)

    masked = jnp.where(gsel > 0.0, choice3, NEG_INF)
    onehots, ids, ws = [], [], []
    for _ in range(TOP_K):
        _, ei = _first_index_of_max(masked, eid, (1, 0), N_EXPERTS)
        hit = eid == ei
        onehots.append(hit)
        ids.append(ei.reshape(1, tm))
        ws.append(jnp.sum(jnp.where(hit, scores3, 0.0), axis=(0, 1), keepdims=True).reshape(1, tm))
        masked = jnp.where(hit, NEG_INF, masked)
    wsum = ws[0]
    for k in range(1, TOP_K):
        wsum = wsum + ws[k]

    sel = jnp.zeros(shape3, F32)
    for hit in onehots:
        sel = jnp.where(hit, 1.0, sel)
    sel2 = sel.reshape(N_EXPERTS, tm)
    before = _dot(sel2.astype(BF16), tri_scr[...]) + cnt_scr[:, :1]
    before3 = before.reshape(shape3)
    ranks = [jnp.sum(jnp.where(hit, before3, 0.0), axis=(0, 1), keepdims=True).reshape(1, tm)
             for hit in onehots]
    cnt_scr[...] = cnt_scr[...] + jnp.sum(sel2, axis=1, keepdims=True)

    idx_ref[...] = jnp.concatenate(ids, axis=0)
    wgt_ref[...] = jnp.concatenate([w / wsum * ROUTED_SCALE for w in ws], axis=0)
    rank_ref[...] = jnp.concatenate(ranks, axis=0).astype(I32)
    cnt_ref[...] = cnt_scr[...]


def _router(h2, w_router_t, bias_col):
    n = h2.shape[0]
    tm = ROUTER_TM
    out_blk = pl.BlockSpec((TOP_K, tm), lambda i: (0, i))
    return pl.pallas_call(
        _router_kernel,
        out_shape=(jax.ShapeDtypeStruct((TOP_K, n), I32), jax.ShapeDtypeStruct((TOP_K, n), F32),
                   jax.ShapeDtypeStruct((TOP_K, n), I32), jax.ShapeDtypeStruct((N_EXPERTS, HEAD_DIM), F32)),
        grid=(n // tm,),
        in_specs=[pl.BlockSpec((tm, D_MODEL), lambda i: (i, 0)),
                  pl.BlockSpec((N_EXPERTS, D_MODEL), lambda i: (0, 0)),
                  pl.BlockSpec((N_EXPERTS, HEAD_DIM), lambda i: (0, 0))],
        out_specs=(out_blk, out_blk, out_blk, pl.BlockSpec((N_EXPERTS, HEAD_DIM), lambda i: (0, 0))),
        scratch_shapes=[pltpu.VMEM((N_EXPERTS, HEAD_DIM), F32), pltpu.VMEM((tm, tm), BF16)],
        compiler_params=_cparams(("arbitrary",)),
        name="router",
    )(h2, w_router_t, bias_col)


def _row_copy(src, src_row, dst, dst_row, sem, rows=1):
    return pltpu.make_async_copy(src.at[pl.ds(src_row, rows)], dst.at[pl.ds(dst_row, rows)], sem)


def _zero_fill_padding(fill_ref, end_ref, xs_hbm, zero_scr, zsem, wait):
    def go(copy):
        copy.wait() if wait else copy.start()

    def per_expert(e, carry):
        def per_row(r, c):
            go(pltpu.make_async_copy(zero_scr.at[pl.ds(0, 1)], xs_hbm.at[pl.ds(r, 1)], zsem))
            return c
        return lax.fori_loop(fill_ref[e], end_ref[e], per_row, carry)

    lax.fori_loop(0, N_EXPERTS, per_expert, 0)
    total = end_ref[N_EXPERTS - 1]
    n_tiles = xs_hbm.shape[0] // FFN_TM
    for tile in range(n_tiles - N_EXPERTS, n_tiles):
        @pl.when(tile * FFN_TM >= total)
        def _(tile=tile):
            go(pltpu.make_async_copy(zero_scr, xs_hbm.at[pl.ds(tile * FFN_TM, FFN_TM)], zsem))


def _step_slots(pos, tm):
    n = pos.shape[1]
    return pos.T.reshape(n // tm, 1, tm * TOP_K)


def _dispatch_kernel(fill_ref, end_ref, pos_ref, h_ref, xs_hbm, zero_scr, sem, zsem):
    i = pl.program_id(0)
    tm = h_ref.shape[0]

    @pl.when(i == 0)
    def _():
        zero_scr[...] = jnp.zeros_like(zero_scr)
        _zero_fill_padding(fill_ref, end_ref, xs_hbm, zero_scr, zsem, wait=False)
        _zero_fill_padding(fill_ref, end_ref, xs_hbm, zero_scr, zsem, wait=True)

    def issue(t, carry):
        for k in range(TOP_K):
            _row_copy(h_ref, t, xs_hbm, pos_ref[0, 0, t * TOP_K + k], sem).start()
        return carry

    lax.fori_loop(0, tm, issue, 0)
    def drain(t, carry):
        _row_copy(h_ref, 0, xs_hbm, 0, sem, rows=TOP_K).wait()
        return carry

    lax.fori_loop(0, tm, drain, 0)


def _dispatch(h2p, pos, seg_fill, seg_end, n_slots):
    n, width = h2p.shape
    tm = DISPATCH_TM
    smem_blk = pl.BlockSpec((1, 1, TOP_K * tm), lambda i, *_: (i, 0, 0), memory_space=pltpu.SMEM)
    return pl.pallas_call(
        _dispatch_kernel,
        out_shape=jax.ShapeDtypeStruct((n_slots, width), h2p.dtype),
        grid_spec=pltpu.PrefetchScalarGridSpec(
            num_scalar_prefetch=2,
            grid=(n // tm,),
            in_specs=[smem_blk, pl.BlockSpec((tm, width), lambda i, *_: (i, 0))],
            out_specs=pl.BlockSpec(memory_space=pl.ANY),
            scratch_shapes=[pltpu.VMEM((FFN_TM, width), h2p.dtype),
                            pltpu.SemaphoreType.DMA(()), pltpu.SemaphoreType.DMA(())],
        ),
        compiler_params=_cparams(("arbitrary",)),
        name="moe_dispatch",
    )(seg_fill, seg_end, _step_slots(pos, tm), h2p)


def _ffn_kernel(te_ref, nv_ref, x_ref, wg_ref, wu_ref, wd_ref, o_ref, wg_bf, wu_bf, wd_bf):
    i = pl.program_id(0)
    nvalid = nv_ref[i]

    @pl.when((i == 0) | (te_ref[i] != te_ref[jnp.maximum(i - 1, 0)]))
    def _():
        wg_bf[...] = wg_ref[...].astype(BF16)
        wu_bf[...] = wu_ref[...].astype(BF16)
        wd_bf[...] = wd_ref[...].astype(BF16)

    @pl.when(nvalid > 0)
    def _():
        x = _unpack_bf16_pairs(x_ref[...])
        hg = _dot(x, wg_bf[...])
        hu = _dot(x, wu_bf[...])
        h = (hg * _sigmoid(hg) * hu).astype(BF16)
        o_ref[...] = _dot(h, wd_bf[...])

    @pl.when(nvalid == 0)
    def _():
        o_ref[...] = jnp.zeros_like(o_ref)


def _expert_ffn(x_sorted, tile_expert, tile_valid, wg, wu, wd, layer):
    n_slots = x_sorted.shape[0]
    tm = FFN_TM
    return pl.pallas_call(
        _ffn_kernel,
        out_shape=jax.ShapeDtypeStruct((n_slots, D_MODEL), F32),
        grid_spec=pltpu.PrefetchScalarGridSpec(
            num_scalar_prefetch=2,
            grid=(n_slots // tm,),
            in_specs=[pl.BlockSpec((tm, D_MODEL // 2), lambda i, te, nv: (i, 0)),
                      pl.BlockSpec((None, None, D_MODEL, EXPERT_FF), lambda i, te, nv: (layer, te[i], 0, 0)),
                      pl.BlockSpec((None, None, D_MODEL, EXPERT_FF), lambda i, te, nv: (layer, te[i], 0, 0)),
                      pl.BlockSpec((None, None, EXPERT_FF, D_MODEL), lambda i, te, nv: (layer, te[i], 0, 0))],
            out_specs=pl.BlockSpec((tm, D_MODEL), lambda i, te, nv: (i, 0)),
            scratch_shapes=[pltpu.VMEM((D_MODEL, EXPERT_FF), BF16), pltpu.VMEM((D_MODEL, EXPERT_FF), BF16),
                            pltpu.VMEM((EXPERT_FF, D_MODEL), BF16)],
        ),
        compiler_params=_cparams(("arbitrary",)),
        name="moe_ffn",
    )(tile_expert, tile_valid, x_sorted, wg, wu, wd)


def _combine_kernel(pos_ref, posn_ref, wgt_ref, x1_ref, h_ref, gt_ref,
                    wsg_ref, wsu_ref, wsd_ref, y_hbm, o_ref, ybuf, sems):
    i = pl.program_id(0)
    nsteps = pl.num_programs(0)
    tm = x1_ref.shape[0]

    def issue(slot, pr):
        def body(t, carry):
            for k in range(TOP_K):
                pltpu.make_async_copy(y_hbm.at[pl.ds(pr[0, 0, t * TOP_K + k], 1)],
                                      ybuf.at[slot, k, pl.ds(t, 1)], sems.at[slot]).start()
            return carry
        lax.fori_loop(0, tm, body, 0)

    slot = i % 2

    @pl.when(i == 0)
    def _():
        issue(0, pos_ref)

    for nxt in range(2):
        @pl.when((i + 1 < nsteps) & (slot != nxt))
        def _(nxt=nxt):
            issue(nxt, posn_ref)

    hb = h_ref[...].astype(BF16)
    hg = _dot(hb, wsg_ref[...])
    hu = _dot(hb, wsu_ref[...])
    shared = _dot((hg * _sigmoid(hg) * hu).astype(BF16), wsd_ref[...])

    for k in range(TOP_K):
        pltpu.make_async_copy(y_hbm.at[pl.ds(0, tm)], ybuf.at[slot, k], sems.at[slot]).wait()

    acc = shared
    for k in range(TOP_K):
        acc = acc + wgt_ref[:, k:k + 1] * ybuf[slot, k]
    o_ref[...] = x1_ref[...] + gt_ref[...] * acc


def _combine(y_sorted, pos, wgt_t, x1, h2, mod5, layer, wsg, wsu, wsd, geo):
    n = x1.shape[0]
    tm = COMBINE_TM
    nsteps = n // tm
    smem_cur = pl.BlockSpec((1, 1, TOP_K * tm), lambda i: (i, 0, 0), memory_space=pltpu.SMEM)
    smem_next = pl.BlockSpec((1, 1, TOP_K * tm), lambda i: (jnp.minimum(i + 1, nsteps - 1), 0, 0),
                             memory_space=pltpu.SMEM)
    slots = _step_slots(pos, tm)

    def gt_map(i):
        b, _ = geo.locate(i * tm)
        return (layer, 5, b, 0, 0)

    row = pl.BlockSpec((tm, D_MODEL), lambda i: (i, 0))
    whole = lambda a, b: pl.BlockSpec((a, b), lambda i: (0, 0))
    return pl.pallas_call(
        _combine_kernel,
        out_shape=jax.ShapeDtypeStruct((n, D_MODEL), F32),
        grid=(nsteps,),
        in_specs=[smem_cur, smem_next,
                  pl.BlockSpec((tm, TOP_K), lambda i: (i, 0)),
                  row, row,
                  pl.BlockSpec((None, None, None, 1, D_MODEL), gt_map),
                  whole(D_MODEL, EXPERT_FF), whole(D_MODEL, EXPERT_FF), whole(EXPERT_FF, D_MODEL),
                  pl.BlockSpec(memory_space=pl.ANY)],
        out_specs=row,
        scratch_shapes=[pltpu.VMEM((2, TOP_K, tm, D_MODEL), F32), pltpu.SemaphoreType.DMA((2,))],
        compiler_params=_cparams(("arbitrary",)),
        name="moe_combine",
    )(slots, slots, wgt_t, x1, h2, mod5, wsg, wsu, wsd, y_sorted)


def _rope_tables(t_max):
    pos = jnp.arange(t_max, dtype=jnp.int32)

    def angles(p, n_pairs):
        inv = ROPE_BASE ** (-jnp.arange(n_pairs, dtype=F32) / n_pairs)
        ang = p.astype(F32)[:, None] * inv[None, :]
        return jnp.cos(ang), jnp.sin(ang)

    c, s = angles(pos, HEAD_DIM // 2)
    cr, sr = angles(pos // GRID_W, HEAD_DIM // 4)
    cc, sc = angles(pos % GRID_W, HEAD_DIM // 4)
    return (jnp.concatenate([c, c], -1), jnp.concatenate([-s, s], -1),
            jnp.concatenate([cr, cr, cc, cc], -1), jnp.concatenate([-sr, sr, -sc, sc], -1))


def _moe_plan(counts, n_slots):
    padded = (counts + FFN_TM - 1) // FFN_TM * FFN_TM
    seg_end = jnp.cumsum(padded)
    seg_start = seg_end - padded
    tile_row0 = jnp.arange(n_slots // FFN_TM, dtype=I32) * FFN_TM
    tile_expert = jnp.minimum(jnp.sum(tile_row0[:, None] >= seg_end[None, :], axis=1), N_EXPERTS - 1).astype(I32)
    tile_valid = jnp.clip(seg_start[tile_expert] + counts[tile_expert] - tile_row0, 0, FFN_TM).astype(I32)
    return (seg_start.astype(I32), (seg_start + counts).astype(I32), seg_end.astype(I32),
            tile_expert, tile_valid)


def _layer(x, mod5, layer, p, tabs, geo):
    n = x.shape[0]
    proj = _inproj(x, mod5, layer, p["g1"], p["w_in"], tabs, p["gains"], geo)
    y_ret = _retention(proj, p["lg_f"], p["lg_b"], p["ret_gn"], geo)
    y_na = _neighbourhood(proj, p["na_tab"], geo)
    y_ga = jnp.concatenate([_flash_group(proj, 0, geo.bp, geo.tp),
                            _flash_group(proj, geo.n_p, geo.bs, geo.ts)], axis=0)
    x1, h2, h2p = _outproj(y_ret, y_na, y_ga, x, p["w_out"], p["na_on"], p["ga_on"], mod5, layer, p["g2"], geo)
    idx, wgt, rank, cnt = _router(h2, p["w_router_t"], p["router_bias"])
    n_slots = n * TOP_K + N_EXPERTS * FFN_TM
    seg_start, seg_fill, seg_end, tile_expert, tile_valid = _moe_plan(cnt[:, 0].astype(I32), n_slots)
    pos = rank
    for e in range(N_EXPERTS):
        pos = pos + jnp.where(idx == e, seg_start[e], 0)
    x_sorted = _dispatch(h2p, pos, seg_fill, seg_end, n_slots)
    y_sorted = _expert_ffn(x_sorted, tile_expert, tile_valid, p["w_eg"], p["w_eu"], p["w_ed"], layer)
    return _combine(y_sorted, pos, wgt.T, x1, h2, mod5, layer, p["w_sg"], p["w_su"], p["w_sd"], geo)


def kernel(x_prompt, x_sample, c_prompt, c_sample, w_ada, b_ada, norm1, w_in, ret_decay_fwd, ret_decay_bwd, ret_norm, na_q_norm, na_k_norm, na_rpb, na_out_norm, ga_q_norm, ga_k_norm, ga_out_norm, w_out, norm2, w_router, router_bias, w_exp_gate, w_exp_up, w_exp_down, w_sh_gate, w_sh_up, w_sh_down):
    bp, tp, d = x_prompt.shape
    bs, ts, _ = x_sample.shape
    depth = w_ada.shape[0]
    geo = Geom(bp, tp, bs, ts)
    assert d == D_MODEL and w_in.shape[-1] == PROJ_W
    for t in (tp, ts):
        assert t % max(PROJ_TM, FLASH_TQ, NA_SLAB_ROWS * GRID_W) == 0 and t % min(FLASH_TK, t) == 0

    x = jnp.concatenate([x_prompt.reshape(bp * tp, d), x_sample.reshape(bs * ts, d)], axis=0)
    c_all = jnp.concatenate([c_prompt, c_sample], axis=0)
    c_pad = jnp.zeros((8 * pl.cdiv(geo.nb, 8), d), F32).at[:geo.nb].set(c_all)
    mod = _adaln(c_pad, w_ada, b_ada)
    rows = mod.shape[1]
    mod5 = mod.reshape(depth, rows, N_MOD, 1, d).transpose(0, 2, 1, 3, 4)
    tabs = _rope_tables(max(tp, ts))

    for l in range(depth):
        p = {
            "g1": norm1[l].reshape(1, d),
            "w_in": w_in[l].astype(BF16),
            "gains": jnp.stack([na_q_norm[l], na_k_norm[l], ga_q_norm[l], ga_k_norm[l]]),
            "lg_f": jax.nn.log_sigmoid(ret_decay_fwd[l].astype(F32)),
            "lg_b": jax.nn.log_sigmoid(ret_decay_bwd[l].astype(F32)),
            "ret_gn": ret_norm[l].reshape(1, RET_W),
            "na_tab": _na_bias_table(na_rpb[l]),
            "na_on": na_out_norm[l].reshape(1, NA_W),
            "ga_on": ga_out_norm[l].reshape(1, GA_W),
            "w_out": w_out[l].astype(BF16),
            "g2": norm2[l].reshape(1, d),
            "w_router_t": w_router[l].astype(F32).T,
            "router_bias": jnp.broadcast_to(router_bias[l].astype(F32)[:, None], (N_EXPERTS, HEAD_DIM)),
            "w_eg": w_exp_gate,
            "w_eu": w_exp_up,
            "w_ed": w_exp_down,
            "w_sg": w_sh_gate[l].astype(BF16),
            "w_su": w_sh_up[l].astype(BF16),
            "w_sd": w_sh_down[l].astype(BF16),
        }
        x = _layer(x, mod5, l, p, tabs, geo)

    y_prompt = x[:geo.n_p].reshape(bp, tp, d)
    y_sample = x[geo.n_p:].reshape(bs, ts, d)
    return (y_prompt, y_sample)
```

```python
import dataclasses
import functools

import numpy as np
import jax
import jax.numpy as jnp
from jax import lax
from jax.experimental import pallas as pl
from jax.experimental.pallas import tpu as pltpu

F32 = jnp.float32
BF16 = jnp.bfloat16
I32 = jnp.int32

D_MODEL = 2048
HEAD_DIM = 128
RET_W = 512
NA_W = 512
GA_W = 1024
GA_KV_W = 256
PROJ_W = 4 * RET_W + 3 * NA_W + GA_W + 2 * GA_KV_W
RET_CHUNK = 128
GRID_W = 64
NA_WIN_R = 8
NA_WIN_C = 16
ROPE_BASE = 10000.0
N_EXPERTS = 64
N_GROUPS = 8
GROUP_SIZE = N_EXPERTS // N_GROUPS
TOPK_GROUPS = 4
TOP_K = 8
EXPERT_FF = 512
ROUTED_SCALE = 2.5
N_MOD = 6
EPS = 1e-6
QK_SCALE = HEAD_DIM ** -0.5
LOG2E = 1.4426950408889634
NEG_INF = float("-inf")
MASK_VALUE = -1e30

VMEM_LIMIT_BYTES = 56 * 1024 * 1024

PROJ_TN = 512
PROJ_TM = 1024
NA_QROWS = 8
NA_SLAB_ROWS = 16
NA_KBLK_ROWS = 4
NA_DOT_ROWS = 256
FLASH_TQ = 1024
FLASH_TK = 4096
FLASH_DOT_ROWS = 256
OUT_TM = 256
ROUTER_TM = 512
DISPATCH_TM = 256
FFN_TM = 512
COMBINE_TM = 128
ADALN_TN = 1024


@dataclasses.dataclass(frozen=True)
class Geom:
    bp: int
    tp: int
    bs: int
    ts: int

    @property
    def n_p(self):
        return self.bp * self.tp

    @property
    def n(self):
        return self.bp * self.tp + self.bs * self.ts

    @property
    def nb(self):
        return self.bp + self.bs

    def locate(self, row):
        in_p = row < self.n_p
        rs = jnp.maximum(row - self.n_p, 0)
        b = jnp.where(in_p, row // self.tp, self.bp + rs // self.ts)
        pos = jnp.where(in_p, row % self.tp, rs % self.ts)
        return b, pos

    def seq_len(self, row):
        return jnp.where(row < self.n_p, self.tp, self.ts)


def _cparams(sem, vmem=VMEM_LIMIT_BYTES):
    return pltpu.CompilerParams(dimension_semantics=sem, vmem_limit_bytes=vmem)


def _dot(a, b):
    return jnp.dot(a, b, preferred_element_type=F32)


def _dot_nt(a, b):
    return lax.dot_general(a, b, (((1,), (1,)), ((), ())), preferred_element_type=F32)


def _sigmoid(x):
    return 1.0 / (1.0 + jnp.exp(-x))


def _rms(x, g):
    return x * lax.rsqrt(jnp.mean(x * x, axis=-1, keepdims=True) + EPS) * g


def _adaln_kernel(c_ref, w_ref, b_ref, o_ref):
    c = c_ref[...]
    s = c * _sigmoid(c)
    o_ref[...] = jnp.dot(s, w_ref[...], preferred_element_type=F32,
                         precision=lax.Precision.HIGHEST) + b_ref[...]


def _adaln(c_pad, w_ada, b_ada):
    depth, d, w = w_ada.shape
    rows = c_pad.shape[0]
    return pl.pallas_call(
        _adaln_kernel,
        out_shape=jax.ShapeDtypeStruct((depth, rows, w), F32),
        grid=(depth, w // ADALN_TN),
        in_specs=[
            pl.BlockSpec((rows, d), lambda l, j: (0, 0)),
            pl.BlockSpec((None, d, ADALN_TN), lambda l, j: (l, 0, j)),
            pl.BlockSpec((None, 1, ADALN_TN), lambda l, j: (l, 0, j)),
        ],
        out_specs=pl.BlockSpec((None, rows, ADALN_TN), lambda l, j: (l, 0, j)),
        compiler_params=_cparams(("parallel", "parallel")),
        name="adaln",
    )(c_pad, w_ada, b_ada.reshape(depth, 1, w))


def _rope_ret(x, cos, sin):
    return x * cos + pltpu.roll(x, HEAD_DIM // 2, 1) * sin


def _rope_axial(x, cos, sin):
    lane = lax.broadcasted_iota(I32, x.shape, 1)
    first = (lane & (HEAD_DIM // 4)) == 0
    rot = jnp.where(first, pltpu.roll(x, HEAD_DIM - HEAD_DIM // 4, 1), pltpu.roll(x, HEAD_DIM // 4, 1))
    return x * cos + rot * sin


def _inproj_kernel(x_ref, sc_ref, sh_ref, g1_ref, w_ref, rc_ref, rs_ref, ac_ref, as_ref, gains_ref,
                   o_ref, h_scr, acc_scr):
    j = pl.program_id(1)

    @pl.when(j == 0)
    def _():
        h = _rms(x_ref[...], g1_ref[...]) * (1.0 + sc_ref[...]) + sh_ref[...]
        h_scr[...] = h.astype(BF16)

    acc_scr[...] = _dot(h_scr[...], w_ref[...])
    heads = PROJ_TN // HEAD_DIM

    def head(hh):
        return acc_scr[:, hh * HEAD_DIM:(hh + 1) * HEAD_DIM]

    def put(hh, val):
        o_ref[:, hh * HEAD_DIM:(hh + 1) * HEAD_DIM] = val.astype(BF16)

    @pl.when(j == 0)
    def _():
        for hh in range(heads):
            put(hh, _rope_ret(head(hh), rc_ref[...], rs_ref[...]))

    @pl.when(j == 1)
    def _():
        for hh in range(heads):
            put(hh, _rope_ret(head(hh), rc_ref[...], rs_ref[...]) * QK_SCALE)

    @pl.when((j == 2) | (j == 6))
    def _():
        o_ref[...] = acc_scr[...].astype(BF16)

    @pl.when(j == 3)
    def _():
        a = acc_scr[...]
        o_ref[...] = (a * _sigmoid(a)).astype(BF16)

    @pl.when(j == 4)
    def _():
        for hh in range(heads):
            put(hh, _rms(head(hh), gains_ref[0:1, :]))

    @pl.when(j == 5)
    def _():
        for hh in range(heads):
            put(hh, _rms(head(hh), gains_ref[1:2, :]))

    @pl.when((j == 7) | (j == 8))
    def _():
        for hh in range(heads):
            put(hh, _rope_axial(_rms(head(hh), gains_ref[2:3, :]), ac_ref[...], as_ref[...])
                * (QK_SCALE * LOG2E))

    @pl.when(j == 9)
    def _():
        for hh in range(2):
            put(hh, _rope_axial(_rms(head(hh), gains_ref[3:4, :]), ac_ref[...], as_ref[...]))
        o_ref[:, 2 * HEAD_DIM:] = acc_scr[:, 2 * HEAD_DIM:].astype(BF16)


def _inproj(x, mod5, layer, g1, w_in_bf, tabs, gains, geo):
    n = x.shape[0]
    tm = PROJ_TM

    def mod_map(chunk):
        def f(i, j):
            b, _ = geo.locate(i * tm)
            return (layer, chunk, b, 0, 0)
        return f

    def tab_map(i, j):
        _, pos = geo.locate(i * tm)
        return (pos // tm, 0)

    mod_spec = lambda chunk: pl.BlockSpec((None, None, None, 1, D_MODEL), mod_map(chunk))
    tab_spec = pl.BlockSpec((tm, HEAD_DIM), tab_map)
    return pl.pallas_call(
        _inproj_kernel,
        out_shape=jax.ShapeDtypeStruct((n, PROJ_W), BF16),
        grid=(n // tm, PROJ_W // PROJ_TN),
        in_specs=[
            pl.BlockSpec((tm, D_MODEL), lambda i, j: (i, 0)),
            mod_spec(1), mod_spec(0),
            pl.BlockSpec((1, D_MODEL), lambda i, j: (0, 0)),
            pl.BlockSpec((D_MODEL, PROJ_TN), lambda i, j: (0, j)),
            tab_spec, tab_spec, tab_spec, tab_spec,
            pl.BlockSpec((4, HEAD_DIM), lambda i, j: (0, 0)),
        ],
        out_specs=pl.BlockSpec((tm, PROJ_TN), lambda i, j: (i, j)),
        scratch_shapes=[pltpu.VMEM((tm, D_MODEL), BF16), pltpu.VMEM((tm, PROJ_TN), F32)],
        compiler_params=_cparams(("parallel", "arbitrary")),
        name="inproj",
    )(x, mod5, mod5, g1, w_in_bf, tabs[0], tabs[1], tabs[2], tabs[3], gains)


def _ret_kernel(lg_ref, q_ref, k_ref, v_ref, *rest, reverse, geo):
    if reverse:
        o_ref, s_scr, tab_scr = rest
    else:
        sg_ref, ob_ref, gn_ref, o_ref, s_scr, tab_scr = rest
    i = pl.program_id(0)
    nchunks = pl.num_programs(0)
    c = RET_CHUNK
    heads = RET_W // HEAD_DIM

    @pl.when(i == 0)
    def _():
        row = lax.broadcasted_iota(I32, (c, c), 0).astype(F32)
        col = lax.broadcasted_iota(I32, (c, c), 1).astype(F32)
        for hh in range(heads):
            lg = lg_ref[hh]
            if reverse:
                diff = col - row
                decay = jnp.where(diff > 0, jnp.exp(lg * jnp.maximum(diff, 0.0)), 0.0)
                qdec = jnp.exp(lg * (c - row))
                kdec = jnp.exp(lg * row)
            else:
                diff = row - col
                decay = jnp.where(diff >= 0, jnp.exp(lg * jnp.maximum(diff, 0.0)), 0.0)
                qdec = jnp.exp(lg * (row + 1.0))
                kdec = jnp.exp(lg * (c - 1.0 - row))
            tab_scr[hh, 0] = decay
            tab_scr[hh, 1] = qdec
            tab_scr[hh, 2] = kdec
            tab_scr[hh, 3] = jnp.exp(jnp.zeros((c, c), F32) + lg * c)

    chunk = (nchunks - 1 - i) if reverse else i
    row0 = chunk * c
    _, pos = geo.locate(row0)
    boundary = (pos + c == geo.seq_len(row0)) if reverse else (pos == 0)

    @pl.when(boundary)
    def _():
        s_scr[...] = jnp.zeros_like(s_scr)

    for hh in range(heads):
        sl = slice(hh * HEAD_DIM, (hh + 1) * HEAD_DIM)
        qh = q_ref[:, sl]
        kh = k_ref[:, sl]
        vh = v_ref[:, sl]
        s = _dot_nt(qh, kh) * tab_scr[hh, 0]
        o = _dot(s.astype(BF16), vh)
        qd = (qh.astype(F32) * tab_scr[hh, 1]).astype(BF16)
        o = o + _dot(qd, s_scr[hh].astype(BF16))
        kd_t = (kh.astype(F32) * tab_scr[hh, 2]).T.astype(BF16)
        s_scr[hh] = s_scr[hh] * tab_scr[hh, 3] + _dot(kd_t, vh)
        if reverse:
            o_ref[:, sl] = o
        else:
            tot = o + ob_ref[:, sl]
            y = _rms(tot, gn_ref[:, sl])
            o_ref[:, sl] = (sg_ref[:, sl].astype(F32) * y).astype(BF16)


def _retention(proj, lg_f, lg_b, ret_gn, geo):
    n = proj.shape[0]
    nchunks = n // RET_CHUNK
    c = RET_CHUNK
    heads = RET_W // HEAD_DIM
    scratch = [pltpu.VMEM((heads, HEAD_DIM, HEAD_DIM), F32), pltpu.VMEM((heads, 4, c, c), F32)]
    smem = pl.BlockSpec(memory_space=pltpu.SMEM)

    def col_spec(colblk, rev):
        if rev:
            return pl.BlockSpec((c, RET_W), lambda i: (nchunks - 1 - i, colblk))
        return pl.BlockSpec((c, RET_W), lambda i: (i, colblk))

    o_bwd = pl.pallas_call(
        functools.partial(_ret_kernel, reverse=True, geo=geo),
        out_shape=jax.ShapeDtypeStruct((n, RET_W), F32),
        grid=(nchunks,),
        in_specs=[smem, col_spec(0, True), col_spec(1, True), col_spec(2, True)],
        out_specs=pl.BlockSpec((c, RET_W), lambda i: (nchunks - 1 - i, 0)),
        scratch_shapes=scratch,
        compiler_params=_cparams(("arbitrary",)),
        name="ret_bwd",
    )(lg_b, proj, proj, proj)
    return pl.pallas_call(
        functools.partial(_ret_kernel, reverse=False, geo=geo),
        out_shape=jax.ShapeDtypeStruct((n, RET_W), BF16),
        grid=(nchunks,),
        in_specs=[smem, col_spec(0, False), col_spec(1, False), col_spec(2, False), col_spec(3, False),
                  pl.BlockSpec((c, RET_W), lambda i: (i, 0)),
                  pl.BlockSpec((1, RET_W), lambda i: (0, 0))],
        out_specs=pl.BlockSpec((c, RET_W), lambda i: (i, 0)),
        scratch_shapes=scratch,
        compiler_params=_cparams(("arbitrary",)),
        name="ret_fwd",
    )(lg_f, proj, proj, proj, proj, o_bwd, ret_gn)


def _na_geometry(i, geo):
    rp, rs = geo.tp // GRID_W, geo.ts // GRID_W
    r0g = i * NA_QROWS
    in_p = r0g < geo.bp * rp
    rsmp = jnp.maximum(r0g - geo.bp * rp, 0)
    rows = jnp.where(in_p, rp, rs)
    lr0 = jnp.where(in_p, r0g % rp, rsmp % rs)
    slab = jnp.clip(lr0 - NA_WIN_R // 2, 0, rows - NA_SLAB_ROWS)
    return rows, lr0, slab, r0g - lr0


def _na_kernel(q_ref, k0, k1, k2, k3, v0, v1, v2, v3, tab_ref, o_ref):
    k = jnp.concatenate([k0[...], k1[...], k2[...], k3[...]], axis=0)
    v = jnp.concatenate([v0[...], v1[...], v2[...], v3[...]], axis=0)
    s = _chunked(lambda x: _dot_nt(x, k), q_ref[...], NA_DOT_ROWS) * QK_SCALE + tab_ref[...]
    e = jnp.exp(s - jnp.max(s, axis=-1, keepdims=True))
    prob = (e / jnp.sum(e, axis=-1, keepdims=True)).astype(BF16)
    o_ref[...] = _chunked(lambda x: _dot(x, v), prob, NA_DOT_ROWS).astype(BF16)


def _na_bias_table(rpb):
    rpb = rpb.astype(F32)
    rows = jnp.stack([rpb[:, NA_WIN_R - 1 - v:2 * NA_WIN_R - 1 - v, :] for v in range(NA_WIN_R)], axis=1)
    width = 2 * GRID_W
    lead = GRID_W - NA_WIN_C
    p = jnp.pad(rows, ((0, 0), (0, 0), (0, 0), (lead, width - lead - (2 * NA_WIN_C - 1))))
    skew = jnp.tile(p, GRID_W)[..., :GRID_W * (width - 1)].reshape(p.shape[:-1] + (GRID_W, width - 1))
    toe = skew[..., GRID_W - 1:]
    c = np.arange(GRID_W)[:, None]
    j = np.arange(GRID_W)[None, :]
    cs = np.clip(c - NA_WIN_C // 2, 0, GRID_W - NA_WIN_C)
    valid = (j >= cs) & (j < cs + NA_WIN_C)
    tab8 = jnp.where(valid[None, None, :, None, :], toe.transpose(0, 1, 3, 2, 4), MASK_VALUE)
    blocks = []
    for d in range(NA_SLAB_ROWS // NA_WIN_R + 1):
        per_row = []
        for p in range(NA_QROWS):
            first = (max(p - NA_WIN_R // 2, 0), p, min(p + NA_WIN_R // 2, NA_WIN_R))[d]
            win = tab8[:, d * (NA_WIN_R // 2) + p - first]
            per_row.append(jnp.pad(win, ((0, 0), (0, 0), (first, NA_SLAB_ROWS - NA_WIN_R - first), (0, 0)),
                                   constant_values=MASK_VALUE))
        blocks.append(jnp.stack(per_row, axis=1))
    return jnp.stack(blocks, axis=1).reshape(rpb.shape[0], len(blocks), NA_QROWS * GRID_W,
                                             NA_SLAB_ROWS * GRID_W)


def _neighbourhood(proj, bias_tab, geo):
    n = proj.shape[0]
    tq = NA_QROWS * GRID_W
    blk = NA_KBLK_ROWS * GRID_W
    nblk = NA_SLAB_ROWS // NA_KBLK_ROWS
    heads = NA_W // HEAD_DIM
    qcol, kcol, vcol = 4 * heads, 5 * heads, 6 * heads

    def kv_spec(col0, m):
        def f(h, i):
            _, _, slab, seq_row0 = _na_geometry(i, geo)
            return ((seq_row0 + slab) // NA_KBLK_ROWS + m, col0 + h)
        return pl.BlockSpec((blk, HEAD_DIM), f)

    def tab_map(h, i):
        _, lr0, slab, _ = _na_geometry(i, geo)
        return (h, (lr0 - slab) // (NA_WIN_R // 2), 0, 0)

    return pl.pallas_call(
        _na_kernel,
        out_shape=jax.ShapeDtypeStruct((n, NA_W), BF16),
        grid=(heads, n // tq),
        in_specs=[pl.BlockSpec((tq, HEAD_DIM), lambda h, i: (i, qcol + h))]
        + [kv_spec(kcol, m) for m in range(nblk)]
        + [kv_spec(vcol, m) for m in range(nblk)]
        + [pl.BlockSpec((None, None, tq, NA_SLAB_ROWS * GRID_W), tab_map)],
        out_specs=pl.BlockSpec((tq, HEAD_DIM), lambda h, i: (i, h)),
        compiler_params=_cparams(("parallel", "parallel")),
        name="natten",
    )(*([proj] * (1 + 2 * nblk)), bias_tab)


def _chunked(fn, x, rows):
    return jnp.concatenate([fn(x[c * rows:(c + 1) * rows]) for c in range(x.shape[0] // rows)], axis=0)


def _flash_kernel(q_ref, k_ref, v_ref, o_ref, m_scr, acc_scr, va_scr):
    ki = pl.program_id(3)
    groups = q_ref.shape[1] // HEAD_DIM
    tk = k_ref.shape[0]

    @pl.when(ki == 0)
    def _():
        m_scr[...] = jnp.full_like(m_scr, NEG_INF)
        acc_scr[...] = jnp.zeros_like(acc_scr)
        lane = lax.broadcasted_iota(I32, (tk, HEAD_DIM), 1)
        va_scr[:, HEAD_DIM:] = jnp.where(lane == 0, 1.0, 0.0).astype(BF16)

    va_scr[:, :HEAD_DIM] = v_ref[...]
    k = k_ref[...]
    va = va_scr[...]
    for gq in range(groups):
        q = q_ref[:, gq * HEAD_DIM:(gq + 1) * HEAD_DIM]
        s = _chunked(lambda x: _dot_nt(x, k), q, FLASH_DOT_ROWS)
        m_prev = m_scr[gq][:, :1]
        m_next = jnp.maximum(m_prev, jnp.max(s, axis=1, keepdims=True))
        alpha = jnp.exp2(m_prev - m_next)
        p = jnp.exp2(s - m_next).astype(BF16)
        acc_scr[gq] = alpha * acc_scr[gq] + _chunked(lambda x: _dot(x, va), p, FLASH_DOT_ROWS)
        m_scr[gq] = jnp.broadcast_to(m_next, m_scr.shape[1:])

    @pl.when(ki == pl.num_programs(3) - 1)
    def _():
        for gq in range(groups):
            acc = acc_scr[gq]
            o_ref[:, gq * HEAD_DIM:(gq + 1) * HEAD_DIM] = (
                acc[:, :HEAD_DIM] / acc[:, HEAD_DIM:HEAD_DIM + 1]).astype(BF16)


def _flash_group(proj, row0, batch, t):
    tq, tk = min(FLASH_TQ, t), min(FLASH_TK, t)
    assert row0 % tq == 0 and row0 % tk == 0 and t % tq == 0 and t % tk == 0
    kv_heads = GA_KV_W // HEAD_DIM
    qw = GA_W // kv_heads
    groups = qw // HEAD_DIM
    qcol0 = (4 * RET_W + 3 * NA_W) // qw
    kcol0 = (4 * RET_W + 3 * NA_W + GA_W) // HEAD_DIM
    vcol0 = kcol0 + kv_heads
    return pl.pallas_call(
        _flash_kernel,
        out_shape=jax.ShapeDtypeStruct((batch * t, GA_W), BF16),
        grid=(batch, kv_heads, t // tq, t // tk),
        in_specs=[
            pl.BlockSpec((tq, qw), lambda b, h, qi, ki: ((row0 + b * t) // tq + qi, qcol0 + h)),
            pl.BlockSpec((tk, HEAD_DIM), lambda b, h, qi, ki: ((row0 + b * t) // tk + ki, kcol0 + h)),
            pl.BlockSpec((tk, HEAD_DIM), lambda b, h, qi, ki: ((row0 + b * t) // tk + ki, vcol0 + h)),
        ],
        out_specs=pl.BlockSpec((tq, qw), lambda b, h, qi, ki: ((b * t) // tq + qi, h)),
        scratch_shapes=[pltpu.VMEM((groups, tq, HEAD_DIM), F32),
                        pltpu.VMEM((groups, tq, 2 * HEAD_DIM), F32),
                        pltpu.VMEM((tk, 2 * HEAD_DIM), BF16)],
        compiler_params=_cparams(("parallel", "parallel", "parallel", "arbitrary")),
        name="flash_gqa",
    )(proj, proj, proj)


def _outproj_kernel(yr_ref, yn_ref, yg_ref, x_ref, w_ref, non_ref, gon_ref, gt_ref, sc_ref, sh_ref, g2_ref,
                    x1_ref, h2_ref, h2p_ref):
    ynn = _rms(yn_ref[...].astype(F32), non_ref[...]).astype(BF16)
    ygn = _rms(yg_ref[...].astype(F32), gon_ref[...]).astype(BF16)
    acc = _dot(yr_ref[...], w_ref[0:RET_W, :])
    acc = acc + _dot(ynn, w_ref[RET_W:RET_W + NA_W, :])
    acc = acc + _dot(ygn, w_ref[RET_W + NA_W:, :])
    x1 = x_ref[...] + gt_ref[...] * acc
    x1_ref[...] = x1
    h2 = _rms(x1, g2_ref[...]) * (1.0 + sc_ref[...]) + sh_ref[...]
    h2_ref[...] = h2
    h2p_ref[...] = _pack_bf16_pairs(h2)


def _pack_bf16_pairs(x):
    w = x.shape[1] // 2
    bits = pltpu.bitcast(x.astype(BF16).astype(F32), jnp.uint32)
    return (bits[:, w:] & jnp.uint32(0xFFFF0000)) | (bits[:, :w] >> 16)


def _unpack_bf16_pairs(p):
    lo = pltpu.bitcast(p << 16, F32)
    hi = pltpu.bitcast(p & jnp.uint32(0xFFFF0000), F32)
    return jnp.concatenate([lo, hi], axis=1).astype(BF16)


def _outproj(y_ret, y_na, y_ga, x, w_out_bf, na_on, ga_on, mod5, layer, g2, geo):
    n = x.shape[0]
    tm = OUT_TM

    def mod_spec(chunk):
        def f(i):
            b, _ = geo.locate(i * tm)
            return (layer, chunk, b, 0, 0)
        return pl.BlockSpec((None, None, None, 1, D_MODEL), f)

    row = lambda w: pl.BlockSpec((tm, w), lambda i: (i, 0))
    return pl.pallas_call(
        _outproj_kernel,
        out_shape=(jax.ShapeDtypeStruct((n, D_MODEL), F32), jax.ShapeDtypeStruct((n, D_MODEL), F32),
                   jax.ShapeDtypeStruct((n, D_MODEL // 2), jnp.uint32)),
        grid=(n // tm,),
        in_specs=[row(RET_W), row(NA_W), row(GA_W), row(D_MODEL),
                  pl.BlockSpec((D_MODEL, D_MODEL), lambda i: (0, 0)),
                  pl.BlockSpec((1, NA_W), lambda i: (0, 0)),
                  pl.BlockSpec((1, GA_W), lambda i: (0, 0)),
                  mod_spec(2), mod_spec(4), mod_spec(3),
                  pl.BlockSpec((1, D_MODEL), lambda i: (0, 0))],
        out_specs=(row(D_MODEL), row(D_MODEL), row(D_MODEL // 2)),
        compiler_params=_cparams(("parallel",)),
        name="outproj",
    )(y_ret, y_na, y_ga, x, w_out_bf, na_on, ga_on, mod5, mod5, mod5, g2)


def _first_index_of_max(vals, ids, axes, sentinel):
    m = vals
    for ax in axes:
        m = jnp.max(m, axis=ax, keepdims=True)
    cand = jnp.where(vals == m, ids, sentinel)
    for ax in axes:
        cand = jnp.min(cand, axis=ax, keepdims=True)
    return m, cand


def _router_kernel(h_ref, w_ref, b_ref, idx_ref, wgt_ref, rank_ref, cnt_ref, cnt_scr, tri_scr):
    i = pl.program_id(0)
    tm = h_ref.shape[0]

    @pl.when(i == 0)
    def _():
        cnt_scr[...] = jnp.zeros_like(cnt_scr)
        r = lax.broadcasted_iota(I32, (tm, tm), 0)
        c = lax.broadcasted_iota(I32, (tm, tm), 1)
        tri_scr[...] = jnp.where(r < c, 1.0, 0.0).astype(BF16)

    logits = lax.dot_general(w_ref[...], h_ref[...], (((1,), (1,)), ((), ())),
                             preferred_element_type=F32, precision=lax.Precision.HIGHEST)
    scores = _sigmoid(logits)
    choice = scores + b_ref[...][:, :1]
    shape3 = (N_GROUPS, GROUP_SIZE, tm)
    choice3 = choice.reshape(shape3)
    scores3 = scores.reshape(shape3)
    sub = lax.broadcasted_iota(I32, shape3, 1)
    grp = lax.broadcasted_iota(I32, shape3, 0)
    eid = grp * GROUP_SIZE + sub

    m1, i1 = _first_index_of_max(choice3, sub, (1,), GROUP_SIZE)
    rest = jnp.where(sub == i1, NEG_INF, choice3)
    m2 = jnp.max(rest, axis=1, keepdims=True)
    gscore = m1 + m2

    gid = lax.broadcasted_iota(I32, gscore.shape, 0)
    gsel = jnp.zeros(gscore.shape, F32)
    for _ in range(TOPK_GROUPS):
        _, gi = _first_index_of_max(gscore, gid, (0,), N_GROUPS)
        hit = gid == gi
        gsel = jnp.where(hit, 1.0, gsel)
        gscore = jnp.where(hit, NEG_INF, gscore)

    masked = jnp.where(gsel > 0.0, choice3, NEG_INF)
    onehots, ids, ws = [], [], []
    for _ in range(TOP_K):
        _, ei = _first_index_of_max(masked, eid, (1, 0), N_EXPERTS)
        hit = eid == ei
        onehots.append(hit)
        ids.append(ei.reshape(1, tm))
        ws.append(jnp.sum(jnp.where(hit, scores3, 0.0), axis=(0, 1), keepdims=True).reshape(1, tm))
        masked = jnp.where(hit, NEG_INF, masked)
    wsum = ws[0]
    for k in range(1, TOP_K):
        wsum = wsum + ws[k]

    sel = jnp.zeros(shape3, F32)
    for hit in onehots:
        sel = jnp.where(hit, 1.0, sel)
    sel2 = sel.reshape(N_EXPERTS, tm)
    before = _dot(sel2.astype(BF16), tri_scr[...]) + cnt_scr[:, :1]
    before3 = before.reshape(shape3)
    ranks = [jnp.sum(jnp.where(hit, before3, 0.0), axis=(0, 1), keepdims=True).reshape(1, tm)
             for hit in onehots]
    cnt_scr[...] = cnt_scr[...] + jnp.sum(sel2, axis=1, keepdims=True)

    idx_ref[...] = jnp.concatenate(ids, axis=0)
    wgt_ref[...] = jnp.concatenate([w / wsum * ROUTED_SCALE for w in ws], axis=0)
    rank_ref[...] = jnp.concatenate(ranks, axis=0).astype(I32)
    cnt_ref[...] = cnt_scr[...]


def _router(h2, w_router_t, bias_col):
    n = h2.shape[0]
    tm = ROUTER_TM
    out_blk = pl.BlockSpec((TOP_K, tm), lambda i: (0, i))
    return pl.pallas_call(
        _router_kernel,
        out_shape=(jax.ShapeDtypeStruct((TOP_K, n), I32), jax.ShapeDtypeStruct((TOP_K, n), F32),
                   jax.ShapeDtypeStruct((TOP_K, n), I32), jax.ShapeDtypeStruct((N_EXPERTS, HEAD_DIM), F32)),
        grid=(n // tm,),
        in_specs=[pl.BlockSpec((tm, D_MODEL), lambda i: (i, 0)),
                  pl.BlockSpec((N_EXPERTS, D_MODEL), lambda i: (0, 0)),
                  pl.BlockSpec((N_EXPERTS, HEAD_DIM), lambda i: (0, 0))],
        out_specs=(out_blk, out_blk, out_blk, pl.BlockSpec((N_EXPERTS, HEAD_DIM), lambda i: (0, 0))),
        scratch_shapes=[pltpu.VMEM((N_EXPERTS, HEAD_DIM), F32), pltpu.VMEM((tm, tm), BF16)],
        compiler_params=_cparams(("arbitrary",)),
        name="router",
    )(h2, w_router_t, bias_col)


def _row_copy(src, src_row, dst, dst_row, sem, rows=1):
    return pltpu.make_async_copy(src.at[pl.ds(src_row, rows)], dst.at[pl.ds(dst_row, rows)], sem)


def _zero_fill_padding(fill_ref, end_ref, xs_hbm, zero_scr, zsem, wait):
    def go(copy):
        copy.wait() if wait else copy.start()

    def per_expert(e, carry):
        def per_row(r, c):
            go(pltpu.make_async_copy(zero_scr.at[pl.ds(0, 1)], xs_hbm.at[pl.ds(r, 1)], zsem))
            return c
        return lax.fori_loop(fill_ref[e], end_ref[e], per_row, carry)

    lax.fori_loop(0, N_EXPERTS, per_expert, 0)
    total = end_ref[N_EXPERTS - 1]
    n_tiles = xs_hbm.shape[0] // FFN_TM
    for tile in range(n_tiles - N_EXPERTS, n_tiles):
        @pl.when(tile * FFN_TM >= total)
        def _(tile=tile):
            go(pltpu.make_async_copy(zero_scr, xs_hbm.at[pl.ds(tile * FFN_TM, FFN_TM)], zsem))


def _step_slots(pos, tm):
    n = pos.shape[1]
    return pos.T.reshape(n // tm, 1, tm * TOP_K)


def _dispatch_kernel(fill_ref, end_ref, pos_ref, h_ref, xs_hbm, zero_scr, sem, zsem):
    i = pl.program_id(0)
    tm = h_ref.shape[0]

    @pl.when(i == 0)
    def _():
        zero_scr[...] = jnp.zeros_like(zero_scr)
        _zero_fill_padding(fill_ref, end_ref, xs_hbm, zero_scr, zsem, wait=False)
        _zero_fill_padding(fill_ref, end_ref, xs_hbm, zero_scr, zsem, wait=True)

    def issue(t, carry):
        for k in range(TOP_K):
            _row_copy(h_ref, t, xs_hbm, pos_ref[0, 0, t * TOP_K + k], sem).start(priority=k % 2)
        return carry

    lax.fori_loop(0, tm, issue, 0)
    def drain(t, carry):
        _row_copy(h_ref, 0, xs_hbm, 0, sem, rows=TOP_K).wait()
        return carry

    lax.fori_loop(0, tm, drain, 0)


def _dispatch(h2p, pos, seg_fill, seg_end, n_slots):
    n, width = h2p.shape
    tm = DISPATCH_TM
    smem_blk = pl.BlockSpec((1, 1, TOP_K * tm), lambda i, *_: (i, 0, 0), memory_space=pltpu.SMEM)
    return pl.pallas_call(
        _dispatch_kernel,
        out_shape=jax.ShapeDtypeStruct((n_slots, width), h2p.dtype),
        grid_spec=pltpu.PrefetchScalarGridSpec(
            num_scalar_prefetch=2,
            grid=(n // tm,),
            in_specs=[smem_blk, pl.BlockSpec((tm, width), lambda i, *_: (i, 0))],
            out_specs=pl.BlockSpec(memory_space=pl.ANY),
            scratch_shapes=[pltpu.VMEM((FFN_TM, width), h2p.dtype),
                            pltpu.SemaphoreType.DMA(()), pltpu.SemaphoreType.DMA(())],
        ),
        compiler_params=_cparams(("arbitrary",)),
        name="moe_dispatch",
    )(seg_fill, seg_end, _step_slots(pos, tm), h2p)


def _ffn_kernel(te_ref, nv_ref, x_ref, wg_ref, wu_ref, wd_ref, o_ref, wg_bf, wu_bf, wd_bf):
    i = pl.program_id(0)
    nvalid = nv_ref[i]

    @pl.when((i == 0) | (te_ref[i] != te_ref[jnp.maximum(i - 1, 0)]))
    def _():
        wg_bf[...] = wg_ref[...].astype(BF16)
        wu_bf[...] = wu_ref[...].astype(BF16)
        wd_bf[...] = wd_ref[...].astype(BF16)

    @pl.when(nvalid > 0)
    def _():
        x = _unpack_bf16_pairs(x_ref[...])
        hg = _dot(x, wg_bf[...])
        hu = _dot(x, wu_bf[...])
        h = (hg * _sigmoid(hg) * hu).astype(BF16)
        o_ref[...] = _dot(h, wd_bf[...])

    @pl.when(nvalid == 0)
    def _():
        o_ref[...] = jnp.zeros_like(o_ref)


def _expert_ffn(x_sorted, tile_expert, tile_valid, wg, wu, wd, layer):
    n_slots = x_sorted.shape[0]
    tm = FFN_TM
    return pl.pallas_call(
        _ffn_kernel,
        out_shape=jax.ShapeDtypeStruct((n_slots, D_MODEL), F32),
        grid_spec=pltpu.PrefetchScalarGridSpec(
            num_scalar_prefetch=2,
            grid=(n_slots // tm,),
            in_specs=[pl.BlockSpec((tm, D_MODEL // 2), lambda i, te, nv: (i, 0)),
                      pl.BlockSpec((None, None, D_MODEL, EXPERT_FF), lambda i, te, nv: (layer, te[i], 0, 0)),
                      pl.BlockSpec((None, None, D_MODEL, EXPERT_FF), lambda i, te, nv: (layer, te[i], 0, 0)),
                      pl.BlockSpec((None, None, EXPERT_FF, D_MODEL), lambda i, te, nv: (layer, te[i], 0, 0))],
            out_specs=pl.BlockSpec((tm, D_MODEL), lambda i, te, nv: (i, 0)),
            scratch_shapes=[pltpu.VMEM((D_MODEL, EXPERT_FF), BF16), pltpu.VMEM((D_MODEL, EXPERT_FF), BF16),
                            pltpu.VMEM((EXPERT_FF, D_MODEL), BF16)],
        ),
        compiler_params=_cparams(("arbitrary",)),
        name="moe_ffn",
    )(tile_expert, tile_valid, x_sorted, wg, wu, wd)


def _combine_kernel(pos_ref, posn_ref, wgt_ref, x1_ref, h_ref, gt_ref,
                    wsg_ref, wsu_ref, wsd_ref, y_hbm, o_ref, ybuf, sems):
    i = pl.program_id(0)
    nsteps = pl.num_programs(0)
    tm = x1_ref.shape[0]

    def issue(slot, pr):
        def body(t, carry):
            for k in range(TOP_K):
                pltpu.make_async_copy(y_hbm.at[pl.ds(pr[0, 0, t * TOP_K + k], 1)],
                                      ybuf.at[slot, k, pl.ds(t, 1)], sems.at[slot]).start()
            return carry
        lax.fori_loop(0, tm, body, 0)

    slot = i % 2

    @pl.when(i == 0)
    def _():
        issue(0, pos_ref)

    for nxt in range(2):
        @pl.when((i + 1 < nsteps) & (slot != nxt))
        def _(nxt=nxt):
            issue(nxt, posn_ref)

    hb = h_ref[...].astype(BF16)
    hg = _dot(hb, wsg_ref[...])
    hu = _dot(hb, wsu_ref[...])
    shared = _dot((hg * _sigmoid(hg) * hu).astype(BF16), wsd_ref[...])

    for k in range(TOP_K):
        pltpu.make_async_copy(y_hbm.at[pl.ds(0, tm)], ybuf.at[slot, k], sems.at[slot]).wait()

    acc = shared
    for k in range(TOP_K):
        acc = acc + wgt_ref[:, k:k + 1] * ybuf[slot, k]
    o_ref[...] = x1_ref[...] + gt_ref[...] * acc


def _combine(y_sorted, pos, wgt_t, x1, h2, mod5, layer, wsg, wsu, wsd, geo):
    n = x1.shape[0]
    tm = COMBINE_TM
    nsteps = n // tm
    smem_cur = pl.BlockSpec((1, 1, TOP_K * tm), lambda i: (i, 0, 0), memory_space=pltpu.SMEM)
    smem_next = pl.BlockSpec((1, 1, TOP_K * tm), lambda i: (jnp.minimum(i + 1, nsteps - 1), 0, 0),
                             memory_space=pltpu.SMEM)
    slots = _step_slots(pos, tm)

    def gt_map(i):
        b, _ = geo.locate(i * tm)
        return (layer, 5, b, 0, 0)

    row = pl.BlockSpec((tm, D_MODEL), lambda i: (i, 0))
    whole = lambda a, b: pl.BlockSpec((a, b), lambda i: (0, 0))
    return pl.pallas_call(
        _combine_kernel,
        out_shape=jax.ShapeDtypeStruct((n, D_MODEL), F32),
        grid=(nsteps,),
        in_specs=[smem_cur, smem_next,
                  pl.BlockSpec((tm, TOP_K), lambda i: (i, 0)),
                  row, row,
                  pl.BlockSpec((None, None, None, 1, D_MODEL), gt_map),
                  whole(D_MODEL, EXPERT_FF), whole(D_MODEL, EXPERT_FF), whole(EXPERT_FF, D_MODEL),
                  pl.BlockSpec(memory_space=pl.ANY)],
        out_specs=row,
        scratch_shapes=[pltpu.VMEM((2, TOP_K, tm, D_MODEL), F32), pltpu.SemaphoreType.DMA((2,))],
        compiler_params=_cparams(("arbitrary",)),
        name="moe_combine",
    )(slots, slots, wgt_t, x1, h2, mod5, wsg, wsu, wsd, y_sorted)


def _rope_tables(t_max):
    pos = jnp.arange(t_max, dtype=jnp.int32)

    def angles(p, n_pairs):
        inv = ROPE_BASE ** (-jnp.arange(n_pairs, dtype=F32) / n_pairs)
        ang = p.astype(F32)[:, None] * inv[None, :]
        return jnp.cos(ang), jnp.sin(ang)

    c, s = angles(pos, HEAD_DIM // 2)
    cr, sr = angles(pos // GRID_W, HEAD_DIM // 4)
    cc, sc = angles(pos % GRID_W, HEAD_DIM // 4)
    return (jnp.concatenate([c, c], -1), jnp.concatenate([-s, s], -1),
            jnp.concatenate([cr, cr, cc, cc], -1), jnp.concatenate([-sr, sr, -sc, sc], -1))


def _moe_plan(counts, n_slots):
    padded = (counts + FFN_TM - 1) // FFN_TM * FFN_TM
    seg_end = jnp.cumsum(padded)
    seg_start = seg_end - padded
    tile_row0 = jnp.arange(n_slots // FFN_TM, dtype=I32) * FFN_TM
    tile_expert = jnp.minimum(jnp.sum(tile_row0[:, None] >= seg_end[None, :], axis=1), N_EXPERTS - 1).astype(I32)
    tile_valid = jnp.clip(seg_start[tile_expert] + counts[tile_expert] - tile_row0, 0, FFN_TM).astype(I32)
    return (seg_start.astype(I32), (seg_start + counts).astype(I32), seg_end.astype(I32),
            tile_expert, tile_valid)


def _layer(x, mod5, layer, p, tabs, geo):
    n = x.shape[0]
    proj = _inproj(x, mod5, layer, p["g1"], p["w_in"], tabs, p["gains"], geo)
    y_ret = _retention(proj, p["lg_f"], p["lg_b"], p["ret_gn"], geo)
    y_na = _neighbourhood(proj, p["na_tab"], geo)
    y_ga = jnp.concatenate([_flash_group(proj, 0, geo.bp, geo.tp),
                            _flash_group(proj, geo.n_p, geo.bs, geo.ts)], axis=0)
    x1, h2, h2p = _outproj(y_ret, y_na, y_ga, x, p["w_out"], p["na_on"], p["ga_on"], mod5, layer, p["g2"], geo)
    idx, wgt, rank, cnt = _router(h2, p["w_router_t"], p["router_bias"])
    n_slots = n * TOP_K + N_EXPERTS * FFN_TM
    seg_start, seg_fill, seg_end, tile_expert, tile_valid = _moe_plan(cnt[:, 0].astype(I32), n_slots)
    pos = rank
    for e in range(N_EXPERTS):
        pos = pos + jnp.where(idx == e, seg_start[e], 0)
    x_sorted = _dispatch(h2p, pos, seg_fill, seg_end, n_slots)
    y_sorted = _expert_ffn(x_sorted, tile_expert, tile_valid, p["w_eg"], p["w_eu"], p["w_ed"], layer)
    return _combine(y_sorted, pos, wgt.T, x1, h2, mod5, layer, p["w_sg"], p["w_su"], p["w_sd"], geo)


def kernel(x_prompt, x_sample, c_prompt, c_sample, w_ada, b_ada, norm1, w_in, ret_decay_fwd, ret_decay_bwd, ret_norm, na_q_norm, na_k_norm, na_rpb, na_out_norm, ga_q_norm, ga_k_norm, ga_out_norm, w_out, norm2, w_router, router_bias, w_exp_gate, w_exp_up, w_exp_down, w_sh_gate, w_sh_up, w_sh_down):
    bp, tp, d = x_prompt.shape
    bs, ts, _ = x_sample.shape
    depth = w_ada.shape[0]
    geo = Geom(bp, tp, bs, ts)
    assert d == D_MODEL and w_in.shape[-1] == PROJ_W
    for t in (tp, ts):
        assert t % max(PROJ_TM, FLASH_TQ, NA_SLAB_ROWS * GRID_W) == 0 and t % min(FLASH_TK, t) == 0

    x = jnp.concatenate([x_prompt.reshape(bp * tp, d), x_sample.reshape(bs * ts, d)], axis=0)
    c_all = jnp.concatenate([c_prompt, c_sample], axis=0)
    c_pad = jnp.zeros((8 * pl.cdiv(geo.nb, 8), d), F32).at[:geo.nb].set(c_all)
    mod = _adaln(c_pad, w_ada, b_ada)
    rows = mod.shape[1]
    mod5 = mod.reshape(depth, rows, N_MOD, 1, d).transpose(0, 2, 1, 3, 4)
    tabs = _rope_tables(max(tp, ts))

    for l in range(depth):
        p = {
            "g1": norm1[l].reshape(1, d),
            "w_in": w_in[l].astype(BF16),
            "gains": jnp.stack([na_q_norm[l], na_k_norm[l], ga_q_norm[l], ga_k_norm[l]]),
            "lg_f": jax.nn.log_sigmoid(ret_decay_fwd[l].astype(F32)),
            "lg_b": jax.nn.log_sigmoid(ret_decay_bwd[l].astype(F32)),
            "ret_gn": ret_norm[l].reshape(1, RET_W),
            "na_tab": _na_bias_table(na_rpb[l]),
            "na_on": na_out_norm[l].reshape(1, NA_W),
            "ga_on": ga_out_norm[l].reshape(1, GA_W),
            "w_out": w_out[l].astype(BF16),
            "g2": norm2[l].reshape(1, d),
            "w_router_t": w_router[l].astype(F32).T,
            "router_bias": jnp.broadcast_to(router_bias[l].astype(F32)[:, None], (N_EXPERTS, HEAD_DIM)),
            "w_eg": w_exp_gate,
            "w_eu": w_exp_up,
            "w_ed": w_exp_down,
            "w_sg": w_sh_gate[l].astype(BF16),
            "w_su": w_sh_up[l].astype(BF16),
            "w_sd": w_sh_down[l].astype(BF16),
        }
        x = _layer(x, mod5, l, p, tabs, geo)

    y_prompt = x[:geo.n_p].reshape(bp, tp, d)
    y_sample = x[geo.n_p:].reshape(bs, ts, d)
    return (y_prompt, y_sample)
```
